```python
import jax, jax.numpy as jnp
from jax import lax
import numpy as np

D_MODEL = 1024
BATCH = 4
SEQ = 4096
DEPTH = 1

GRID_W = 64
CTX_LEN = 256
POOL_WINDOWS = (2, 4, 8, 16)
POOL_GROUPS = len(POOL_WINDOWS)
POOL_GROUP_DIM = 128
POOL_WIDTH = POOL_GROUPS * POOL_GROUP_DIM
HG_HEADS = 4
HG_KEY_DIM = 128
HG_VAL_DIM = 128
HG_QK_WIDTH = HG_HEADS * HG_KEY_DIM
HG_V_WIDTH = HG_HEADS * HG_VAL_DIM
HG_CHUNK = 16
N_BRANCH = 2
IN_WIDTH = POOL_WIDTH + 3 * HG_QK_WIDTH + 2 * HG_V_WIDTH + N_BRANCH * D_MODEL
PEER_HEADS = 8
PEER_NKEYS = 128
PEER_EXPERTS = PEER_NKEYS * PEER_NKEYS
PEER_KEY_HALF = 128
PEER_TOPK = 16
PEER_BLOCK = 128
N_MOD = 6
EPS = 1e-6

kernel_name = 'hybrid_pool_hgrn2_peer_dit_block'


def rms_norm(x, g):
    xf = x.astype(jnp.float32)
    y = xf * lax.rsqrt(jnp.mean(xf * xf, axis=-1, keepdims=True) + EPS)
    return (y * g.astype(jnp.float32)).astype(x.dtype)


def modulate(x, g, shift, scale):
    return rms_norm(x, g) * (1 + scale) + shift


def split_proj(h, w_in):
    z = h @ w_in
    idx = list(np.cumsum([POOL_WIDTH, HG_QK_WIDTH, HG_V_WIDTH, HG_QK_WIDTH, HG_QK_WIDTH, HG_V_WIDTH, D_MODEL]))
    return jnp.split(z, idx, axis=-1)


def box_sum(x, w, axis):
    n = x.shape[axis]
    cs = jnp.cumsum(x.astype(jnp.float32), axis=axis)
    pad = [(0, 0)] * x.ndim
    pad[axis] = (1, 0)
    cs = jnp.pad(cs, pad)
    t = jnp.arange(n)
    lo = jnp.clip(t - w // 2, 0, n)
    hi = jnp.clip(t - w // 2 + w, 0, n)
    s = jnp.take(cs, hi, axis=axis) - jnp.take(cs, lo, axis=axis)
    return s, (hi - lo).astype(jnp.float32)


def pool_branch(p, pool_w, pool_scale, on_grid):
    B, L, _ = p.shape
    outs = []
    for g, win in enumerate(POOL_WINDOWS):
        pg = p[..., g * POOL_GROUP_DIM:(g + 1) * POOL_GROUP_DIM]
        if on_grid:
            rows = L // GRID_W
            s, cr = box_sum(pg.reshape(B, rows, GRID_W, POOL_GROUP_DIM), win, 1)
            s, cc = box_sum(s, win, 2)
            mean = (s / (cr[:, None] * cc[None, :])[None, :, :, None]).reshape(B, L, POOL_GROUP_DIM)
        else:
            s, cnt = box_sum(pg, win, 1)
            mean = s / cnt[None, :, None]
        outs.append((mean.astype(p.dtype) - pg) @ pool_w[g])
    return jnp.concatenate(outs, axis=-1) * pool_scale


def chunk_scan(q, k, v, log_f, s0):
    B, L, H, _ = q.shape
    n = L // HG_CHUNK
    r = lambda t: t.reshape(B, n, HG_CHUNK, H, t.shape[-1]).astype(jnp.float32)
    qc, kc, vc, gc = r(q), r(k), r(v), r(log_f)
    b = jnp.cumsum(gc, axis=2)
    mask = jnp.tril(jnp.ones((HG_CHUNK, HG_CHUNK), bool))
    diff = b[:, :, :, None] - b[:, :, None, :]
    decay = jnp.exp(jnp.where(mask[:, :, None, None], diff, -jnp.inf))
    attn = jnp.einsum('bntshk,bnthk,bnshk->bntsh', decay, qc, kc)
    o_intra = jnp.einsum('bntsh,bnshv->bnthv', attn, vc)
    b_last = b[:, :, -1]
    k_dec = kc * jnp.exp(b_last[:, :, None] - b)
    u = jnp.einsum('bnshk,bnshv->bnhkv', k_dec, vc)
    a = jnp.exp(b_last)

    def step(S, inp):
        a_n, u_n = inp
        return a_n[..., None] * S + u_n, S

    s_final, s_prev = lax.scan(step, s0, (jnp.moveaxis(a, 1, 0), jnp.moveaxis(u, 1, 0)))
    s_prev = jnp.moveaxis(s_prev, 0, 1)
    o_inter = jnp.einsum('bnthk,bnhkv->bnthv', qc * jnp.exp(b), s_prev)
    return (o_intra + o_inter).reshape(B, L, H, -1), s_final


def hgrn2_direction(q, i, f_raw, lb, s0, reverse):
    f = lb + (1 - lb) * jax.nn.sigmoid(f_raw.astype(jnp.float32))
    log_f = jnp.log(f)
    k = 1 - f
    if reverse:
        q, k, i, log_f = (jnp.flip(t, axis=1) for t in (q, k, i, log_f))
    o, s = chunk_scan(q, k, i, log_f, s0)
    if reverse:
        o = jnp.flip(o, axis=1)
    return o, s


def hgrn2_branch(parts, lb, norm_g, init_states):
    heads = lambda t: t.reshape(t.shape[0], t.shape[1], HG_HEADS, -1)
    q, i, f_fwd, f_bwd, og = (heads(t) for t in parts[1:6])
    B, L = q.shape[0], q.shape[1]
    if init_states is None:
        zero = jnp.zeros((B, HG_HEADS, HG_KEY_DIM, HG_VAL_DIM), jnp.float32)
        init_states = (zero, zero)
    lb = lb.reshape(2, HG_HEADS, HG_KEY_DIM)
    o_f, s_f = hgrn2_direction(q, i, f_fwd, lb[0], init_states[0], False)
    o_b, s_b = hgrn2_direction(q, i, f_bwd, lb[1], init_states[1], True)
    o = rms_norm(o_f + o_b, norm_g) * jax.nn.silu(og.astype(jnp.float32))
    return o.reshape(B, L, HG_V_WIDTH).astype(parts[0].dtype), (s_f, s_b)


def merge_branches(parts, y_pool, y_hg, w_a, w_b, w_o):
    g_pool, g_hg = parts[6], parts[7]
    m = jax.nn.sigmoid(g_pool) * (y_pool @ w_a) + jax.nn.sigmoid(g_hg) * (y_hg @ w_b)
    return m @ w_o


def peer_ffn(h, w_q, sub_keys, u, v):
    B, L, D = h.shape
    blocks = h.reshape(-1, PEER_BLOCK, D)

    def block(hb):
        T = hb.shape[0]
        q = (hb @ w_q).reshape(T, PEER_HEADS, 2, PEER_KEY_HALF)
        s = jnp.einsum('thpc,hpnc->thpn', q, sub_keys).astype(jnp.float32)
        v1, i1 = lax.top_k(s[:, :, 0], PEER_TOPK)
        v2, i2 = lax.top_k(s[:, :, 1], PEER_TOPK)
        cand = (v1[..., :, None] + v2[..., None, :]).reshape(T, PEER_HEADS, PEER_TOPK * PEER_TOPK)
        cidx = (i1[..., :, None] * PEER_NKEYS + i2[..., None, :]).reshape(T, PEER_HEADS, PEER_TOPK * PEER_TOPK)
        top_s, pos = lax.top_k(cand, PEER_TOPK)
        eidx = jnp.take_along_axis(cidx, pos, axis=-1)
        gate = jax.nn.softmax(top_s, axis=-1)
        act = jax.nn.gelu(jnp.einsum('td,thkd->thk', hb, u[eidx]).astype(jnp.float32), approximate=False)
        return jnp.einsum('thk,thkd->td', (gate * act).astype(hb.dtype), v[eidx])

    return lax.map(block, blocks).reshape(B, L, D)


def setup_inputs(seed: int = 0) -> dict:
    key = jax.random.key(seed)
    ks = jax.random.split(key, 24)
    nrm = lambda k, shape, s: jax.random.normal(k, shape, jnp.float32) * s
    D = D_MODEL
    return {
        'x': nrm(ks[0], (BATCH, SEQ, D), 1.0),
        'c': nrm(ks[1], (BATCH, D), 1.0),
        'ctx': nrm(ks[2], (BATCH, CTX_LEN, D), 1.0),
        'c_ctx': nrm(ks[3], (D,), 1.0),
        'w_mod': nrm(ks[4], (DEPTH, D, N_MOD * D), 0.5 * D ** -0.5),
        'b_mod': nrm(ks[5], (DEPTH, N_MOD * D), 0.01),
        'norm1_g': 1.0 + nrm(ks[6], (DEPTH, D), 0.01),
        'norm2_g': 1.0 + nrm(ks[7], (DEPTH, D), 0.01),
        'w_in': nrm(ks[8], (DEPTH, D, IN_WIDTH), D ** -0.5),
        'pool_w': nrm(ks[9], (DEPTH, POOL_GROUPS, POOL_GROUP_DIM, POOL_GROUP_DIM), POOL_GROUP_DIM ** -0.5),
        'pool_scale': 1.0 + nrm(ks[10], (DEPTH, POOL_WIDTH), 0.01),
        'hg_lb_logits': nrm(ks[11], (2, DEPTH + 1, HG_QK_WIDTH), 0.1),
        'hg_norm_g': 1.0 + nrm(ks[12], (DEPTH, HG_VAL_DIM), 0.01),
        'w_a': nrm(ks[13], (DEPTH, POOL_WIDTH, D), POOL_WIDTH ** -0.5),
        'w_b': nrm(ks[14], (DEPTH, HG_V_WIDTH, D), HG_V_WIDTH ** -0.5),
        'w_o': nrm(ks[15], (DEPTH, D, D), D ** -0.5),
        'peer_wq': nrm(ks[16], (DEPTH, D, PEER_HEADS * 2 * PEER_KEY_HALF), D ** -0.5),
        'peer_keys': nrm(ks[17], (DEPTH, PEER_HEADS, 2, PEER_NKEYS, PEER_KEY_HALF), PEER_KEY_HALF ** -0.5),
        'peer_u': nrm(ks[18], (DEPTH, PEER_EXPERTS, D), D ** -0.5),
        'peer_v': nrm(ks[19], (DEPTH, PEER_EXPERTS, D), PEER_HEADS ** -0.5),
        'final_g': 1.0 + nrm(ks[20], (D,), 0.01),
    }


def reference(x, c, ctx, c_ctx, w_mod, b_mod, norm1_g, norm2_g, w_in, pool_w, pool_scale,
              hg_lb_logits, hg_norm_g, w_a, w_b, w_o, peer_wq, peer_keys, peer_u, peer_v, final_g):
    hs = ctx
    lbs = jnp.cumsum(jax.nn.softmax(hg_lb_logits.astype(jnp.float32), axis=1), axis=1)
    for l in range(DEPTH):
        last = l == DEPTH - 1
        mod = (jax.nn.silu(c) @ w_mod[l] + b_mod[l])[:, None, :]
        mod_c = (jax.nn.silu(c_ctx) @ w_mod[l] + b_mod[l])[None, None, :]
        sh1, sc1, g1, sh2, sc2, g2 = jnp.split(mod, N_MOD, axis=-1)
        csh1, csc1, cg1, csh2, csc2, cg2 = jnp.split(mod_c, N_MOD, axis=-1)
        lb = lbs[:, l]

        pc = split_proj(modulate(hs, norm1_g[l], csh1, csc1), w_in[l])
        pl = split_proj(modulate(x, norm1_g[l], sh1, sc1), w_in[l])
        yc_hg, ctx_states = hgrn2_branch(pc, lb, hg_norm_g[l], None)
        yl_hg, _ = hgrn2_branch(pl, lb, hg_norm_g[l], ctx_states)
        yl_pool = pool_branch(pl[0], pool_w[l], pool_scale[l], True)
        x = x + g1 * merge_branches(pl, yl_pool, yl_hg, w_a[l], w_b[l], w_o[l])

        x = x + g2 * peer_ffn(modulate(x, norm2_g[l], sh2, sc2), peer_wq[l], peer_keys[l], peer_u[l], peer_v[l])

        if not last:
            yc_pool = pool_branch(pc[0], pool_w[l], pool_scale[l], False)
            hs = hs + cg1 * merge_branches(pc, yc_pool, yc_hg, w_a[l], w_b[l], w_o[l])
            hs = hs + cg2 * peer_ffn(modulate(hs, norm2_g[l], csh2, csc2), peer_wq[l], peer_keys[l], peer_u[l], peer_v[l])
    return rms_norm(x, final_g)
```

```python
import functools
import math

import jax
import jax.numpy as jnp
from jax import lax
from jax.experimental import pallas as pl
from jax.experimental.pallas import tpu as pltpu

F32 = jnp.float32
BF16 = jnp.bfloat16
I32 = jnp.int32

EPS = 1e-6
GRID_W = 64
POOL_WINDOWS = (2, 4, 8, 16)
POOL_PAD = 8
GROUP = 128
HG_HEADS = 4
HG_CHUNK = 16
HG_BLOCK = 128
HG_MID = 8
PEER_HEADS = 8
PEER_NKEYS = 128
PEER_TOPK = 16
N_MOD = 6
W_PITCH = 136
VMEM_LIMIT = 56 * 1024 * 1024

_NT = (((1,), (1,)), ((), ()))
_SQRT_HALF = math.sqrt(0.5)


def _dot(a, b):
    return jnp.dot(a, b, preferred_element_type=F32)


def _dot_nt(a, b):
    return lax.dot_general(a, b, _NT, preferred_element_type=F32)


def _hi_lo(a):
    hi = a.astype(BF16)
    return hi, (a - hi.astype(F32)).astype(BF16)


def _sigmoid(x):
    return 1.0 / (1.0 + jnp.exp(-x))


def _rms(x):
    return x * lax.rsqrt(jnp.mean(x * x, axis=-1, keepdims=True) + EPS)


def _params(sem):
    return pltpu.CompilerParams(dimension_semantics=sem, vmem_limit_bytes=VMEM_LIMIT)


def _mod_kernel(c_ref, w_ref, b_ref, o_ref):
    c = c_ref[...]
    s = c * _sigmoid(c)
    s_hi, s_lo = _hi_lo(s)
    w_hi, w_lo = _hi_lo(w_ref[...])
    o_ref[...] = _dot(s_hi, w_hi) + (_dot(s_hi, w_lo) + _dot(s_lo, w_hi)) + b_ref[...]


def _mod_call(cvec, w_mod, b_mod):
    rows, d = cvec.shape
    n = w_mod.shape[1]
    tn = n // 4
    return pl.pallas_call(
        _mod_kernel,
        grid=(n // tn,),
        in_specs=[pl.BlockSpec((rows, d), lambda j: (0, 0)),
                  pl.BlockSpec((d, tn), lambda j: (0, j)),
                  pl.BlockSpec((1, tn), lambda j: (0, j))],
        out_specs=pl.BlockSpec((rows, tn), lambda j: (0, j)),
        out_shape=jax.ShapeDtypeStruct((rows, n), F32),
        compiler_params=_params(("arbitrary",)),
    )(cvec, w_mod, b_mod)


def _inproj_kernel(x_ref, sh_ref, sc_ref, g_ref, w_ref, o_ref, h_scr):
    @pl.when(pl.program_id(1) == 0)
    def _():
        h = _rms(x_ref[...]) * g_ref[...] * (1.0 + sc_ref[0]) + sh_ref[0]
        h_scr[...] = h.astype(BF16)

    o_ref[...] = _dot(h_scr[...], w_ref[...])


def _inproj_call(x2d, mod3, sh_row, sc_row, rows_per_mod, g, w16, col0, ncols, tm, tn):
    r, d = x2d.shape
    if rows_per_mod is None:
        sh_map = lambda i, j: (sh_row, 0, 0)
        sc_map = lambda i, j: (sc_row, 0, 0)
    else:
        sh_map = lambda i, j: (sh_row + N_MOD * ((i * tm) // rows_per_mod), 0, 0)
        sc_map = lambda i, j: (sc_row + N_MOD * ((i * tm) // rows_per_mod), 0, 0)
    return pl.pallas_call(
        _inproj_kernel,
        grid=(r // tm, ncols),
        in_specs=[pl.BlockSpec((tm, d), lambda i, j: (i, 0)),
                  pl.BlockSpec((1, 1, d), sh_map),
                  pl.BlockSpec((1, 1, d), sc_map),
                  pl.BlockSpec((1, d), lambda i, j: (0, 0)),
                  pl.BlockSpec((d, tn), lambda i, j: (0, j + col0))],
        out_specs=pl.BlockSpec((tm, tn), lambda i, j: (i, j)),
        out_shape=jax.ShapeDtypeStruct((r, ncols * tn), F32),
        scratch_shapes=[pltpu.VMEM((tm, d), BF16)],
        compiler_params=_params(("arbitrary", "arbitrary")),
    )(x2d, mod3, mod3, g, w16)


def _pool_kernel(p_ref, w_ref, sc_ref, o_ref, pad_scr, tmp_scr):
    gw, pp = GRID_W, POOL_PAD
    pad_scr[...] = jnp.zeros(pad_scr.shape, F32)
    r_i = lax.broadcasted_iota(I32, (gw, gw, GROUP), 0)
    c_i = lax.broadcasted_iota(I32, (gw, gw, GROUP), 1)
    for g, win in enumerate(POOL_WINDOWS):
        half = win // 2
        pg = p_ref[:, g * GROUP:(g + 1) * GROUP].reshape(gw, gw, GROUP)
        pad_scr[pp:pp + gw, pp:pp + gw, :] = pg
        acc = None
        for k in range(-half, win - half):
            sl = pad_scr[pp + k:pp + k + gw, :, :]
            acc = sl if acc is None else acc + sl
        tmp_scr[...] = acc
        acc = None
        for k in range(-half, win - half):
            sl = tmp_scr[:, pp + k:pp + k + gw, :]
            acc = sl if acc is None else acc + sl
        cr = jnp.minimum(r_i - half + win, gw) - jnp.maximum(r_i - half, 0)
        cc = jnp.minimum(c_i - half + win, gw) - jnp.maximum(c_i - half, 0)
        mean = acc / (cr * cc).astype(F32)
        d = (mean - pg).reshape(gw * gw, GROUP)
        y = _dot(d.astype(BF16), w_ref[g].astype(BF16)) * sc_ref[:, g * GROUP:(g + 1) * GROUP]
        o_ref[:, g * GROUP:(g + 1) * GROUP] = y.astype(BF16)


def _pool_call(z, pool_w, pool_scale, batch, seq):
    width = len(POOL_WINDOWS) * GROUP
    ext = GRID_W + 2 * POOL_PAD
    return pl.pallas_call(
        _pool_kernel,
        grid=(batch,),
        in_specs=[pl.BlockSpec((seq, width), lambda b: (b, 0)),
                  pl.BlockSpec(pool_w.shape, lambda b: (0, 0, 0)),
                  pl.BlockSpec((1, width), lambda b: (0, 0))],
        out_specs=pl.BlockSpec((seq, width), lambda b: (b, 0)),
        out_shape=jax.ShapeDtypeStruct((batch * seq, width), BF16),
        scratch_shapes=[pltpu.VMEM((ext, ext, GROUP), F32), pltpu.VMEM((GRID_W, ext, GROUP), F32)],
        compiler_params=_params(("arbitrary",)),
    )(z, pool_w, pool_scale)


def _hgrn_scan(d, f_ref, q_ref, v_ref, lb, st0, o_scr, seq, emit_out):
    t, c = HG_BLOCK, HG_CHUNK
    nblk, nch = seq // t, t // c
    rev = d == 1
    row = lax.broadcasted_iota(I32, (t, t), 0)
    col = lax.broadcasted_iota(I32, (t, t), 1)
    sh = c.bit_length() - 1
    same = (row >> sh) == (col >> sh)
    tri = (col >= row) if rev else (col <= row)
    mid = ((row >> sh) << sh) + HG_MID
    to_mid = (col >= mid) if rev else (col <= mid)
    m_all = jnp.concatenate([jnp.where(same & tri, 1.0, 0.0),
                             jnp.where(same, 1.0, 0.0),
                             jnp.where(same & to_mid, 1.0, 0.0)], axis=0).astype(BF16)
    causal = same & tri
    lane_chunk = col >> sh

    def block(i, st):
        blk = (nblk - 1 - i) if rev else i
        rows = pl.ds(pl.multiple_of(blk * t, t), t)
        f = lb + (1.0 - lb) * _sigmoid(f_ref[rows, :])
        g = jnp.log(f)
        k = 1.0 - f
        g_hi, g_lo = _hi_lo(g)
        b = _dot(m_all, g_hi) + _dot(m_all, g_lo)
        b_loc, b_tot, b_mid = b[0:t], b[t:2 * t], b[2 * t:3 * t]
        v = v_ref[rows, :]
        kd16 = (k * jnp.exp(b_tot - b_loc)).astype(BF16)
        vt = v.T
        a_all = jnp.exp(b_tot)
        if emit_out:
            q = q_ref[rows, :]
            a16 = (q * jnp.exp(b_loc)).astype(BF16)
            qd16 = (q * jnp.exp(b_loc - b_mid)).astype(BF16)
            kg16 = (k * jnp.exp(b_mid - b_loc)).astype(BF16)
            attn = jnp.where(causal, _dot_nt(qd16, kg16), 0.0)
            o = _dot(attn.astype(BF16), v.astype(BF16))
            o_inter = [None] * nch
        for n in (range(nch - 1, -1, -1) if rev else range(nch)):
            if emit_out:
                o_inter[n] = _dot_nt(a16[n * c:(n + 1) * c], st.astype(BF16))
            ut = _dot(jnp.where(lane_chunk == n, vt, 0.0).astype(BF16), kd16)
            st = st * a_all[n * c:n * c + 1, :] + ut
        if emit_out:
            o = o + jnp.concatenate(o_inter, axis=0)
            if rev:
                o_scr[rows, :] += o
            else:
                o_scr[rows, :] = o
        return st

    return lax.fori_loop(0, nblk, block, st0)


def _lower_bounds(lbl_ref):
    out = []
    for d in range(2):
        l0, l1 = lbl_ref[2 * d:2 * d + 1, :], lbl_ref[2 * d + 1:2 * d + 2, :]
        m = jnp.maximum(l0, l1)
        e0, e1 = jnp.exp(l0 - m), jnp.exp(l1 - m)
        out.append(e0 / (e0 + e1))
    return out


def _hgrn_ctx_kernel(v_ref, ff_ref, fb_ref, lbl_ref, sf_ref, sb_ref, *, seq):
    lbs = _lower_bounds(lbl_ref)
    zero = jnp.zeros((GROUP, GROUP), F32)
    sf_ref[0, 0] = _hgrn_scan(0, ff_ref, None, v_ref, lbs[0], zero, None, seq, False)
    sb_ref[0, 0] = _hgrn_scan(1, fb_ref, None, v_ref, lbs[1], zero, None, seq, False)


def _hgrn_kernel(q_ref, v_ref, ff_ref, fb_ref, og_ref, lbl_ref, ng_ref, sf_ref, sb_ref, y_ref, o_scr, *, seq):
    lbs = _lower_bounds(lbl_ref)
    _hgrn_scan(0, ff_ref, q_ref, v_ref, lbs[0], sf_ref[0, 0], o_scr, seq, True)
    _hgrn_scan(1, fb_ref, q_ref, v_ref, lbs[1], sb_ref[0, 0], o_scr, seq, True)
    og = og_ref[...]
    y = _rms(o_scr[...]) * ng_ref[...] * (og * _sigmoid(og))
    y_ref[...] = y.astype(BF16)


def _hgrn_ctx_call(zc, lbl, batch, seq):
    col = lambda k: (lambda b, h: (b, k * HG_HEADS + h))
    st_spec = pl.BlockSpec((1, 1, GROUP, GROUP), lambda b, h: (b, h, 0, 0))
    st_shape = jax.ShapeDtypeStruct((batch, HG_HEADS, GROUP, GROUP), F32)
    return pl.pallas_call(
        functools.partial(_hgrn_ctx_kernel, seq=seq),
        grid=(batch, HG_HEADS),
        in_specs=[pl.BlockSpec((seq, GROUP), col(0)),
                  pl.BlockSpec((seq, GROUP), col(1)),
                  pl.BlockSpec((seq, GROUP), col(2)),
                  pl.BlockSpec((4, GROUP), lambda b, h: (0, h))],
        out_specs=[st_spec, st_spec],
        out_shape=[st_shape, st_shape],
        compiler_params=_params(("arbitrary", "arbitrary")),
    )(zc, zc, zc, lbl)


def _hgrn_call(z, lbl, norm_g, sf, sb, batch, seq):
    col = lambda k: (lambda b, h: (b, k * HG_HEADS + h))
    st_spec = pl.BlockSpec((1, 1, GROUP, GROUP), lambda b, h: (b, h, 0, 0))
    return pl.pallas_call(
        functools.partial(_hgrn_kernel, seq=seq),
        grid=(batch, HG_HEADS),
        in_specs=[pl.BlockSpec((seq, GROUP), col(1)),
                  pl.BlockSpec((seq, GROUP), col(2)),
                  pl.BlockSpec((seq, GROUP), col(3)),
                  pl.BlockSpec((seq, GROUP), col(4)),
                  pl.BlockSpec((seq, GROUP), col(5)),
                  pl.BlockSpec((4, GROUP), lambda b, h: (0, h)),
                  pl.BlockSpec((1, GROUP), lambda b, h: (0, 0)),
                  st_spec, st_spec],
        out_specs=pl.BlockSpec((seq, GROUP), lambda b, h: (b, h)),
        out_shape=jax.ShapeDtypeStruct((batch * seq, HG_HEADS * GROUP), BF16),
        scratch_shapes=[pltpu.VMEM((seq, GROUP), F32)],
        compiler_params=_params(("arbitrary", "arbitrary")),
    )(z, z, z, z, z, lbl, norm_g, sf, sb)


def _merge_kernel(x_ref, yp_ref, yh_ref, gp_ref, gh_ref, g1_ref, sh_ref, sc_ref, ng_ref,
                  wa_ref, wb_ref, wo_ref, wqt_ref, keys_ref, x1_ref, h2_ref, st_ref):
    m = (_sigmoid(gp_ref[...]) * _dot(yp_ref[...], wa_ref[...])
         + _sigmoid(gh_ref[...]) * _dot(yh_ref[...], wb_ref[...]))
    x1 = x_ref[...] + g1_ref[0] * _dot(m.astype(BF16), wo_ref[...])
    x1_ref[...] = x1
    h2 = _rms(x1) * ng_ref[...] * (1.0 + sc_ref[0]) + sh_ref[0]
    h2_ref[...] = h2.astype(BF16)
    qt = _dot(wqt_ref[...], h2.T.astype(BF16))
    for hp in range(2 * PEER_HEADS):
        st_ref[hp] = _dot(keys_ref[hp], qt[hp * GROUP:(hp + 1) * GROUP].astype(BF16))


def _merge_call(x2d, ypool, yhg, z, mod3, norm2_g, wa16, wb16, wo16, wqt16, keys16, seq, tm):
    r, d = x2d.shape
    gcol0 = (z.shape[1] - 2 * d) // d
    modrow = lambda k: (lambda i: (k + N_MOD * ((i * tm) // seq), 0, 0))
    const2 = lambda i: (0, 0)
    nhp = 2 * PEER_HEADS
    return pl.pallas_call(
        _merge_kernel,
        grid=(r // tm,),
        in_specs=[pl.BlockSpec((tm, d), lambda i: (i, 0)),
                  pl.BlockSpec((tm, ypool.shape[1]), lambda i: (i, 0)),
                  pl.BlockSpec((tm, yhg.shape[1]), lambda i: (i, 0)),
                  pl.BlockSpec((tm, d), lambda i: (i, gcol0)),
                  pl.BlockSpec((tm, d), lambda i: (i, gcol0 + 1)),
                  pl.BlockSpec((1, 1, d), modrow(2)),
                  pl.BlockSpec((1, 1, d), modrow(3)),
                  pl.BlockSpec((1, 1, d), modrow(4)),
                  pl.BlockSpec((1, d), const2),
                  pl.BlockSpec(wa16.shape, const2),
                  pl.BlockSpec(wb16.shape, const2),
                  pl.BlockSpec(wo16.shape, const2),
                  pl.BlockSpec(wqt16.shape, const2),
                  pl.BlockSpec(keys16.shape, lambda i: (0, 0, 0))],
        out_specs=[pl.BlockSpec((tm, d), lambda i: (i, 0)),
                   pl.BlockSpec((tm, d), lambda i: (i, 0)),
                   pl.BlockSpec((nhp, PEER_NKEYS, tm), lambda i: (0, 0, i))],
        out_shape=[jax.ShapeDtypeStruct((r, d), F32),
                   jax.ShapeDtypeStruct((r, d), BF16),
                   jax.ShapeDtypeStruct((nhp, PEER_NKEYS, r), F32)],
        compiler_params=_params(("arbitrary",)),
    )(x2d, ypool, yhg, z, z, mod3, mod3, mod3, norm2_g, wa16, wb16, wo16, wqt16, keys16)


def _top16(v, iota_n):
    vals, idxs = [], []
    for _ in range(PEER_TOPK):
        m = jnp.max(v, axis=0, keepdims=True)
        idx = jnp.min(jnp.where(v == m, iota_n, float(PEER_NKEYS)), axis=0, keepdims=True)
        v = jnp.where(iota_n == idx, -jnp.inf, v)
        vals.append(m)
        idxs.append(idx)
    return jnp.concatenate(vals, axis=0), jnp.concatenate(idxs, axis=0)


def _topk_kernel(s_ref, i1_ref, i2_ref, gt_ref, i1_scr, i2_scr, gt_scr):
    tb = s_ref.shape[2]
    kk = PEER_TOPK
    iota_n = lax.broadcasted_iota(I32, (PEER_NKEYS, tb), 0).astype(F32)
    nrow = kk + (kk - 1) * 8
    r = lax.broadcasted_iota(I32, (nrow, tb), 0)
    a_i = jnp.where(r < kk, 0, 1 + ((r - kk) >> 3))
    b_i = jnp.where(r < kk, r, (r - kk) & 7)
    valid = (a_i + 1) * (b_i + 1) <= kk
    pos = (a_i * kk + b_i).astype(F32)

    def head(h, carry):
        v1, n1 = _top16(s_ref[2 * h], iota_n)
        v2, n2 = _top16(s_ref[2 * h + 1], iota_n)
        cand = [v1[0:1] + v2]
        eid = [n1[0:1] * float(PEER_NKEYS) + n2]
        for a in range(1, kk):
            cand.append(v1[a:a + 1] + v2[0:8])
            eid.append(n1[a:a + 1] * float(PEER_NKEYS) + n2[0:8])
        cand = jnp.where(valid, jnp.concatenate(cand, axis=0), -jnp.inf)
        eid = jnp.concatenate(eid, axis=0)
        tops, es = [], []
        for _ in range(kk):
            m = jnp.max(cand, axis=0, keepdims=True)
            p = jnp.min(jnp.where(cand == m, pos, 1e9), axis=0, keepdims=True)
            sel = pos == p
            es.append(jnp.max(jnp.where(sel, eid, -1.0), axis=0, keepdims=True))
            cand = jnp.where(sel, -jnp.inf, cand)
            tops.append(m)
        top = jnp.concatenate(tops, axis=0)
        e = jnp.concatenate(es, axis=0)
        ex = jnp.exp(top - top[0:1])
        gate = ex / jnp.sum(ex, axis=0, keepdims=True)
        e1 = jnp.floor(e * (1.0 / PEER_NKEYS))
        rows = pl.ds(pl.multiple_of(h * kk, kk), kk)
        i1_scr[rows, :] = e1
        i2_scr[rows, :] = e - e1 * float(PEER_NKEYS)
        gt_scr[rows, :] = gate
        return carry

    lax.fori_loop(0, PEER_HEADS, head, 0)
    i1_ref[...] = i1_scr[...].T.astype(I32)
    i2_ref[...] = i2_scr[...].T.astype(I32)
    gt_ref[...] = gt_scr[...].T


def _topk_call(st, tb):
    nhp, nk, r = st.shape
    slots = PEER_HEADS * PEER_TOPK
    out_spec = pl.BlockSpec((tb, slots), lambda i: (i, 0))
    return pl.pallas_call(
        _topk_kernel,
        grid=(r // tb,),
        in_specs=[pl.BlockSpec((nhp, nk, tb), lambda i: (0, 0, i))],
        out_specs=[out_spec, out_spec, out_spec],
        out_shape=[jax.ShapeDtypeStruct((r, slots), I32),
                   jax.ShapeDtypeStruct((r, slots), I32),
                   jax.ShapeDtypeStruct((r, slots), F32)],
        scratch_shapes=[pltpu.VMEM((slots, tb), F32)] * 3,
        compiler_params=_params(("arbitrary",)),
    )(st)


def _peer_kernel(h2_ref, i1_ref, i2_ref, gt_ref, u_ref, v_ref, x1_ref, g2_ref, fg_ref, o_ref, w_scr, acc_scr):
    tb = h2_ref.shape[0]
    eb = u_ref.shape[0]
    e = pl.program_id(1)
    nk = PEER_NKEYS

    @pl.when(e == 0)
    def _build():
        iota_s = lax.broadcasted_iota(I32, (nk, nk), 0)

        def body(t8, carry):
            for rr in range(8):
                t = t8 * 8 + rr
                i1r = i1_ref[pl.ds(t, 1), :]
                i2r = i2_ref[pl.ds(t, 1), :]
                gtr = gt_ref[pl.ds(t, 1), :]
                g1t = jnp.where(iota_s == i1r, gtr, 0.0).astype(BF16)
                p2t = jnp.where(iota_s == i2r, 1.0, 0.0).astype(BF16)
                w_scr[pl.ds(pl.multiple_of(t * W_PITCH, 8), nk), :] = _dot_nt(g1t, p2t)
            return carry

        lax.fori_loop(0, tb // 8, body, 0)
        acc_scr[...] = jnp.zeros(acc_scr.shape, F32)

    a = _dot_nt(h2_ref[...], u_ref[...])
    pieces = []
    for k in range(eb // nk):
        n1 = e * (eb // nk) + k
        wn = w_scr[pl.ds(n1, tb, stride=W_PITCH), :]
        ak = a[:, k * nk:(k + 1) * nk]
        act = 0.5 * ak * (1.0 + lax.erf(ak * _SQRT_HALF))
        pieces.append((wn * act).astype(BF16))
    acc_scr[...] += _dot(jnp.concatenate(pieces, axis=1), v_ref[...])

    @pl.when(e == pl.num_programs(1) - 1)
    def _fin():
        x2 = x1_ref[...] + g2_ref[0] * acc_scr[...]
        o_ref[...] = _rms(x2) * fg_ref[...]


def _peer_call(h2, i1, i2, gt, u16, v16, x1, mod3, final_g, seq, tb, eb):
    r, d = h2.shape
    ne = u16.shape[0]
    slots = i1.shape[1]
    tok = lambda i, e: (i, 0)
    return pl.pallas_call(
        _peer_kernel,
        grid=(r // tb, ne // eb),
        in_specs=[pl.BlockSpec((tb, d), tok),
                  pl.BlockSpec((tb, slots), tok),
                  pl.BlockSpec((tb, slots), tok),
                  pl.BlockSpec((tb, slots), tok),
                  pl.BlockSpec((eb, d), lambda i, e: (e, 0)),
                  pl.BlockSpec((eb, d), lambda i, e: (e, 0)),
                  pl.BlockSpec((tb, d), tok),
                  pl.BlockSpec((1, 1, d), lambda i, e: (5 + N_MOD * ((i * tb) // seq), 0, 0)),
                  pl.BlockSpec((1, d), lambda i, e: (0, 0))],
        out_specs=pl.BlockSpec((tb, d), tok),
        out_shape=jax.ShapeDtypeStruct((r, d), F32),
        scratch_shapes=[pltpu.VMEM((tb * W_PITCH, PEER_NKEYS), F32), pltpu.VMEM((tb, d), F32)],
        compiler_params=_params(("arbitrary", "arbitrary")),
    )(h2, i1, i2, gt, u16, v16, x1, mod3, final_g)


def kernel(x, c, ctx, c_ctx, w_mod, b_mod, norm1_g, norm2_g, w_in, pool_w, pool_scale, hg_lb_logits, hg_norm_g,
           w_a, w_b, w_o, peer_wq, peer_keys, peer_u, peer_v, final_g):
    batch, seq, d = x.shape
    ctx_len = ctx.shape[1]
    assert w_mod.shape[0] == 1 and seq == GRID_W * GRID_W and hg_lb_logits.shape[1] == 2
    hw = HG_HEADS * GROUP

    cvec = jnp.concatenate([c, c_ctx[None, :], jnp.zeros((8 - batch - 1, d), F32)], axis=0)
    mod = _mod_call(cvec, w_mod[0], b_mod[0][None, :])
    mod3 = mod.reshape(8 * N_MOD, 1, d)

    w_in16 = w_in[0].astype(BF16)
    g1n = norm1_g[0][None, :]
    x2d = x.reshape(batch * seq, d)
    z = _inproj_call(x2d, mod3, 0, 1, seq, g1n, w_in16, 0, w_in16.shape[1] // 1024, 1024, 1024)
    zc = _inproj_call(ctx.reshape(batch * ctx_len, d), mod3, N_MOD * batch, N_MOD * batch + 1, None, g1n,
                      w_in16, 1, 2, 256, 1024)

    lbl = hg_lb_logits.reshape(2 * hg_lb_logits.shape[1], hw)
    sf, sb = _hgrn_ctx_call(zc, lbl, batch, ctx_len)
    yhg = _hgrn_call(z, lbl, hg_norm_g[0][None, :], sf, sb, batch, seq)
    ypool = _pool_call(z, pool_w[0], pool_scale[0][None, :], batch, seq)

    nhp = 2 * PEER_HEADS
    x1, h2, st = _merge_call(x2d, ypool, yhg, z, mod3, norm2_g[0][None, :],
                             w_a[0].astype(BF16), w_b[0].astype(BF16), w_o[0].astype(BF16),
                             peer_wq[0].T.astype(BF16),
                             peer_keys[0].reshape(nhp, PEER_NKEYS, GROUP).astype(BF16), seq, 256)
    i1, i2, gt = _topk_call(st, 128)
    out = _peer_call(h2, i1, i2, gt, peer_u[0].astype(BF16), peer_v[0].astype(BF16), x1, mod3,
                     final_g[None, :], seq, 256, 1024)
    return out.reshape(batch, seq, d)
```

```python
import functools
import math

import jax
import jax.numpy as jnp
from jax import lax
from jax.experimental import pallas as pl
from jax.experimental.pallas import tpu as pltpu

F32 = jnp.float32
BF16 = jnp.bfloat16
I32 = jnp.int32

EPS = 1e-6
GRID_W = 64
POOL_WINDOWS = (2, 4, 8, 16)
POOL_PAD = 8
GROUP = 128
HG_HEADS = 4
HG_CHUNK = 16
HG_BLOCK = 128
HG_MID = 8
PEER_HEADS = 8
PEER_NKEYS = 128
PEER_TOPK = 16
N_MOD = 6
W_PITCH = 136
VMEM_LIMIT = 60 * 1024 * 1024

_NT = (((1,), (1,)), ((), ()))
_SQRT_HALF = math.sqrt(0.5)


def _dot(a, b):
    return jnp.dot(a, b, preferred_element_type=F32)


def _dot_nt(a, b):
    return lax.dot_general(a, b, _NT, preferred_element_type=F32)


def _hi_lo(a):
    hi = a.astype(BF16)
    return hi, (a - hi.astype(F32)).astype(BF16)


def _sigmoid(x):
    return 1.0 / (1.0 + jnp.exp(-x))


def _rms(x):
    return x * lax.rsqrt(jnp.mean(x * x, axis=-1, keepdims=True) + EPS)


def _params(sem):
    return pltpu.CompilerParams(dimension_semantics=sem, vmem_limit_bytes=VMEM_LIMIT)


def _mod_kernel(c_ref, w_ref, b_ref, o_ref):
    c = c_ref[...]
    s = c * _sigmoid(c)
    s_hi, s_lo = _hi_lo(s)
    w_hi, w_lo = _hi_lo(w_ref[...])
    o_ref[...] = _dot(s_hi, w_hi) + (_dot(s_hi, w_lo) + _dot(s_lo, w_hi)) + b_ref[...]


def _mod_call(cvec, w_mod, b_mod):
    rows, d = cvec.shape
    n = w_mod.shape[1]
    tn = n // 4
    return pl.pallas_call(
        _mod_kernel,
        grid=(n // tn,),
        in_specs=[pl.BlockSpec((rows, d), lambda j: (0, 0)),
                  pl.BlockSpec((d, tn), lambda j: (0, j)),
                  pl.BlockSpec((1, tn), lambda j: (0, j))],
        out_specs=pl.BlockSpec((rows, tn), lambda j: (0, j)),
        out_shape=jax.ShapeDtypeStruct((rows, n), F32),
        name="mod",
        compiler_params=_params(("arbitrary",)),
    )(cvec, w_mod, b_mod)


def _inproj_kernel(x_ref, sh_ref, sc_ref, g_ref, w_ref, o_ref, h_scr):
    @pl.when(pl.program_id(1) == 0)
    def _():
        h = _rms(x_ref[...]) * g_ref[...] * (1.0 + sc_ref[0]) + sh_ref[0]
        h_scr[...] = h.astype(BF16)

    o_ref[...] = _dot(h_scr[...], w_ref[...])


def _inproj_call(x2d, mod3, sh_row, sc_row, rows_per_mod, g, w16, col0, ncols, tm, tn):
    r, d = x2d.shape
    if rows_per_mod is None:
        sh_map = lambda i, j: (sh_row, 0, 0)
        sc_map = lambda i, j: (sc_row, 0, 0)
    else:
        sh_map = lambda i, j: (sh_row + N_MOD * ((i * tm) // rows_per_mod), 0, 0)
        sc_map = lambda i, j: (sc_row + N_MOD * ((i * tm) // rows_per_mod), 0, 0)
    return pl.pallas_call(
        _inproj_kernel,
        grid=(r // tm, ncols),
        in_specs=[pl.BlockSpec((tm, d), lambda i, j: (i, 0)),
                  pl.BlockSpec((1, 1, d), sh_map),
                  pl.BlockSpec((1, 1, d), sc_map),
                  pl.BlockSpec((1, d), lambda i, j: (0, 0)),
                  pl.BlockSpec((d, tn), lambda i, j: (0, j + col0))],
        out_specs=pl.BlockSpec((tm, tn), lambda i, j: (i, j)),
        out_shape=jax.ShapeDtypeStruct((r, ncols * tn), F32),
        scratch_shapes=[pltpu.VMEM((tm, d), BF16)],
        name="inproj",
        compiler_params=_params(("arbitrary", "arbitrary")),
    )(x2d, mod3, mod3, g, w16)


def _pool_kernel(p_ref, w_ref, sc_ref, o_ref, pad_scr, tmp_scr):
    gw, pp = GRID_W, POOL_PAD
    pad_scr[...] = jnp.zeros(pad_scr.shape, F32)
    r_i = lax.broadcasted_iota(I32, (gw, gw, GROUP), 0)
    c_i = lax.broadcasted_iota(I32, (gw, gw, GROUP), 1)
    for g, win in enumerate(POOL_WINDOWS):
        half = win // 2
        pg = p_ref[:, g * GROUP:(g + 1) * GROUP].reshape(gw, gw, GROUP)
        pad_scr[pp:pp + gw, pp:pp + gw, :] = pg
        acc = None
        for k in range(-half, win - half):
            sl = pad_scr[pp + k:pp + k + gw, :, :]
            acc = sl if acc is None else acc + sl
        tmp_scr[...] = acc
        acc = None
        for k in range(-half, win - half):
            sl = tmp_scr[:, pp + k:pp + k + gw, :]
            acc = sl if acc is None else acc + sl
        cr = jnp.minimum(r_i - half + win, gw) - jnp.maximum(r_i - half, 0)
        cc = jnp.minimum(c_i - half + win, gw) - jnp.maximum(c_i - half, 0)
        mean = acc / (cr * cc).astype(F32)
        d = (mean - pg).reshape(gw * gw, GROUP)
        y = _dot(d.astype(BF16), w_ref[g].astype(BF16)) * sc_ref[:, g * GROUP:(g + 1) * GROUP]
        o_ref[:, g * GROUP:(g + 1) * GROUP] = y.astype(BF16)


def _pool_call(z, pool_w, pool_scale, batch, seq):
    width = len(POOL_WINDOWS) * GROUP
    ext = GRID_W + 2 * POOL_PAD
    return pl.pallas_call(
        _pool_kernel,
        grid=(batch,),
        in_specs=[pl.BlockSpec((seq, width), lambda b: (b, 0)),
                  pl.BlockSpec(pool_w.shape, lambda b: (0, 0, 0)),
                  pl.BlockSpec((1, width), lambda b: (0, 0))],
        out_specs=pl.BlockSpec((seq, width), lambda b: (b, 0)),
        out_shape=jax.ShapeDtypeStruct((batch * seq, width), BF16),
        scratch_shapes=[pltpu.VMEM((ext, ext, GROUP), F32), pltpu.VMEM((GRID_W, ext, GROUP), F32)],
        name="pool",
        compiler_params=_params(("arbitrary",)),
    )(z, pool_w, pool_scale)


def _hgrn_masks(rev):
    t, c = HG_BLOCK, HG_CHUNK
    row = lax.broadcasted_iota(I32, (t, t), 0)
    col = lax.broadcasted_iota(I32, (t, t), 1)
    sh = c.bit_length() - 1
    same = (row >> sh) == (col >> sh)
    tri = (col >= row) if rev else (col <= row)
    mid = ((row >> sh) << sh) + HG_MID
    to_mid = (col >= mid) if rev else (col <= mid)
    m_all = jnp.concatenate([jnp.where(same & tri, 1.0, 0.0),
                             jnp.where(same, 1.0, 0.0),
                             jnp.where(same & to_mid, 1.0, 0.0)], axis=0).astype(BF16)
    return m_all, same & tri, col >> sh


def _hgrn_blocks(chains, masks, emit_out):
    t, c = HG_BLOCK, HG_CHUNK
    nch = t // c
    nc = len(chains)
    revs = [ch[0] for ch in chains]
    qs = [ch[2] for ch in chains]
    vs = [ch[3] for ch in chains]
    sts = [ch[5] for ch in chains]
    ks, bs = [], []
    for rev, x_f, _, _, lb, _ in chains:
        f = lb + (1.0 - lb) * _sigmoid(x_f)
        g_hi, g_lo = _hi_lo(jnp.log(f))
        m_all = masks[rev][0]
        ks.append(1.0 - f)
        bs.append(_dot(m_all, g_hi) + _dot(m_all, g_lo))
    b_loc = [b[0:t] for b in bs]
    b_tot = [b[t:2 * t] for b in bs]
    b_mid = [b[2 * t:3 * t] for b in bs]
    kd16 = [(ks[i] * jnp.exp(b_tot[i] - b_loc[i])).astype(BF16) for i in range(nc)]
    vt = [v.T for v in vs]
    a_all = [jnp.exp(b) for b in b_tot]
    os_ = [None] * nc
    if emit_out:
        a16 = [(qs[i] * jnp.exp(b_loc[i])).astype(BF16) for i in range(nc)]
        qd16 = [(qs[i] * jnp.exp(b_loc[i] - b_mid[i])).astype(BF16) for i in range(nc)]
        kg16 = [(ks[i] * jnp.exp(b_mid[i] - b_loc[i])).astype(BF16) for i in range(nc)]
        attn = [jnp.where(masks[revs[i]][1], _dot_nt(qd16[i], kg16[i]), 0.0) for i in range(nc)]
        os_ = [_dot(attn[i].astype(BF16), vs[i].astype(BF16)) for i in range(nc)]
        o_inter = [[None] * nch for _ in range(nc)]
    for step in range(nch):
        for i in range(nc):
            n = nch - 1 - step if revs[i] else step
            if emit_out:
                o_inter[i][n] = _dot_nt(a16[i][n * c:(n + 1) * c], sts[i].astype(BF16))
            ut = _dot(jnp.where(masks[revs[i]][2] == n, vt[i], 0.0).astype(BF16), kd16[i])
            sts[i] = sts[i] * a_all[i][n * c:n * c + 1, :] + ut
    if emit_out:
        os_ = [os_[i] + jnp.concatenate(o_inter[i], axis=0) for i in range(nc)]
    return list(zip(os_, sts))


def _lower_bounds(lbl_ref):
    out = []
    for d in range(2):
        l0, l1 = lbl_ref[2 * d:2 * d + 1, :], lbl_ref[2 * d + 1:2 * d + 2, :]
        m = jnp.maximum(l0, l1)
        e0, e1 = jnp.exp(l0 - m), jnp.exp(l1 - m)
        out.append(e0 / (e0 + e1))
    return out


def _hgrn_run(nheads, seq, lbl_ref, f_refs, q_ref, v_ref, st_scr, o_scrs):
    t = HG_BLOCK
    nblk = seq // t
    masks = (_hgrn_masks(False), _hgrn_masks(True))
    lbs = _lower_bounds(lbl_ref)
    emit_out = o_scrs is not None

    def body(i, carry):
        chains = []
        for d in range(2):
            blk = (nblk - 1 - i) if d else i
            rows = pl.ds(pl.multiple_of(blk * t, t), t)
            for h in range(nheads):
                ls = slice(h * GROUP, (h + 1) * GROUP)
                q = q_ref[rows, ls] if emit_out else None
                chains.append((d, h, rows, ls, f_refs[d][rows, ls], q, v_ref[rows, ls], st_scr[d * nheads + h]))
        results = _hgrn_blocks([(d, x_f, q, v, lbs[d][:, ls], st) for d, h, rows, ls, x_f, q, v, st in chains],
                               masks, emit_out)
        for (d, h, rows, ls, *_), (o, st) in zip(chains, results):
            st_scr[d * nheads + h] = st
            if emit_out:
                o_scrs[d][rows, ls] = o
        return carry

    lax.fori_loop(0, nblk, body, 0)


def _hgrn_ctx_kernel(v_ref, ff_ref, fb_ref, lbl_ref, sf_ref, sb_ref, st_scr, *, seq):
    nheads = sf_ref.shape[1]
    st_scr[...] = jnp.zeros(st_scr.shape, F32)
    _hgrn_run(nheads, seq, lbl_ref, (ff_ref, fb_ref), None, v_ref, st_scr, None)
    sf_ref[0] = st_scr[0:nheads]
    sb_ref[0] = st_scr[nheads:2 * nheads]


def _hgrn_kernel(q_ref, v_ref, ff_ref, fb_ref, og_ref, lbl_ref, ng_ref, sf_ref, sb_ref, y_ref,
                 st_scr, of_scr, ob_scr, *, seq):
    nheads = sf_ref.shape[1]
    st_scr[0:nheads] = sf_ref[0]
    st_scr[nheads:2 * nheads] = sb_ref[0]
    _hgrn_run(nheads, seq, lbl_ref, (ff_ref, fb_ref), q_ref, v_ref, st_scr, (of_scr, ob_scr))
    rb = 4 * HG_BLOCK

    def epilogue(i, carry):
        rows = pl.ds(pl.multiple_of(i * rb, rb), rb)
        for h in range(nheads):
            ls = slice(h * GROUP, (h + 1) * GROUP)
            og = og_ref[rows, ls]
            y = _rms(of_scr[rows, ls] + ob_scr[rows, ls]) * ng_ref[...] * (og * _sigmoid(og))
            y_ref[rows, ls] = y.astype(BF16)
        return carry

    lax.fori_loop(0, seq // rb, epilogue, 0)


def _hgrn_ctx_call(zc, lbl, batch, seq):
    hw = HG_HEADS * GROUP
    st_spec = pl.BlockSpec((1, HG_HEADS, GROUP, GROUP), lambda b: (b, 0, 0, 0))
    st_shape = jax.ShapeDtypeStruct((batch, HG_HEADS, GROUP, GROUP), F32)
    return pl.pallas_call(
        functools.partial(_hgrn_ctx_kernel, seq=seq),
        grid=(batch,),
        in_specs=[pl.BlockSpec((seq, hw), lambda b: (b, 0)),
                  pl.BlockSpec((seq, hw), lambda b: (b, 1)),
                  pl.BlockSpec((seq, hw), lambda b: (b, 2)),
                  pl.BlockSpec((4, hw), lambda b: (0, 0))],
        out_specs=[st_spec, st_spec],
        out_shape=[st_shape, st_shape],
        scratch_shapes=[pltpu.VMEM((2 * HG_HEADS, GROUP, GROUP), F32)],
        name="hgrn_ctx",
        compiler_params=_params(("arbitrary",)),
    )(zc, zc, zc, lbl)


def _hgrn_call(z, lbl, norm_g, sf, sb, batch, seq, nheads):
    ngrp = HG_HEADS // nheads
    w = nheads * GROUP
    col = lambda k: (lambda b, h: (b, k * ngrp + h))
    big = lambda k: pl.BlockSpec((seq, w), col(k), pipeline_mode=pl.Buffered(1))
    st_spec = pl.BlockSpec((1, nheads, GROUP, GROUP), lambda b, h: (b, h, 0, 0))
    return pl.pallas_call(
        functools.partial(_hgrn_kernel, seq=seq),
        grid=(batch, ngrp),
        in_specs=[big(1), big(2), big(3), big(4), big(5),
                  pl.BlockSpec((4, w), lambda b, h: (0, h)),
                  pl.BlockSpec((1, GROUP), lambda b, h: (0, 0)),
                  st_spec, st_spec],
        out_specs=pl.BlockSpec((seq, w), lambda b, h: (b, h)),
        out_shape=jax.ShapeDtypeStruct((batch * seq, HG_HEADS * GROUP), BF16),
        scratch_shapes=[pltpu.VMEM((2 * nheads, GROUP, GROUP), F32),
                        pltpu.VMEM((seq, w), F32), pltpu.VMEM((seq, w), F32)],
        name="hgrn",
        compiler_params=_params(("arbitrary", "arbitrary")),
    )(z, z, z, z, z, lbl, norm_g, sf, sb)


def _merge_kernel(x_ref, yp_ref, yh_ref, gp_ref, gh_ref, g1_ref, sh_ref, sc_ref, ng_ref,
                  wa_ref, wb_ref, wo_ref, wqt_ref, keys_ref, x1_ref, h2_ref, st_ref):
    m = (_sigmoid(gp_ref[...]) * _dot(yp_ref[...], wa_ref[...])
         + _sigmoid(gh_ref[...]) * _dot(yh_ref[...], wb_ref[...]))
    x1 = x_ref[...] + g1_ref[0] * _dot(m.astype(BF16), wo_ref[...])
    x1_ref[...] = x1
    h2 = _rms(x1) * ng_ref[...] * (1.0 + sc_ref[0]) + sh_ref[0]
    h2_ref[...] = h2.astype(BF16)
    qt = _dot(wqt_ref[...], h2.T.astype(BF16))
    for hp in range(2 * PEER_HEADS):
        st_ref[hp] = _dot(keys_ref[hp], qt[hp * GROUP:(hp + 1) * GROUP].astype(BF16))


def _merge_call(x2d, ypool, yhg, z, mod3, norm2_g, wa16, wb16, wo16, wqt16, keys16, seq, tm):
    r, d = x2d.shape
    gcol0 = (z.shape[1] - 2 * d) // d
    modrow = lambda k: (lambda i: (k + N_MOD * ((i * tm) // seq), 0, 0))
    const2 = lambda i: (0, 0)
    nhp = 2 * PEER_HEADS
    return pl.pallas_call(
        _merge_kernel,
        grid=(r // tm,),
        in_specs=[pl.BlockSpec((tm, d), lambda i: (i, 0)),
                  pl.BlockSpec((tm, ypool.shape[1]), lambda i: (i, 0)),
                  pl.BlockSpec((tm, yhg.shape[1]), lambda i: (i, 0)),
                  pl.BlockSpec((tm, d), lambda i: (i, gcol0)),
                  pl.BlockSpec((tm, d), lambda i: (i, gcol0 + 1)),
                  pl.BlockSpec((1, 1, d), modrow(2)),
                  pl.BlockSpec((1, 1, d), modrow(3)),
                  pl.BlockSpec((1, 1, d), modrow(4)),
                  pl.BlockSpec((1, d), const2),
                  pl.BlockSpec(wa16.shape, const2),
                  pl.BlockSpec(wb16.shape, const2),
                  pl.BlockSpec(wo16.shape, const2),
                  pl.BlockSpec(wqt16.shape, const2),
                  pl.BlockSpec(keys16.shape, lambda i: (0, 0, 0))],
        out_specs=[pl.BlockSpec((tm, d), lambda i: (i, 0)),
                   pl.BlockSpec((tm, d), lambda i: (i, 0)),
                   pl.BlockSpec((nhp, PEER_NKEYS, tm), lambda i: (0, 0, i))],
        out_shape=[jax.ShapeDtypeStruct((r, d), F32),
                   jax.ShapeDtypeStruct((r, d), BF16),
                   jax.ShapeDtypeStruct((nhp, PEER_NKEYS, r), F32)],
        name="merge",
        compiler_params=_params(("arbitrary",)),
    )(x2d, ypool, yhg, z, z, mod3, mod3, mod3, norm2_g, wa16, wb16, wo16, wqt16, keys16)


def _top16(v, iota_n):
    vals, idxs = [], []
    for _ in range(PEER_TOPK):
        m = jnp.max(v, axis=0, keepdims=True)
        idx = jnp.min(jnp.where(v == m, iota_n, float(PEER_NKEYS)), axis=0, keepdims=True)
        v = jnp.where(iota_n == idx, -jnp.inf, v)
        vals.append(m)
        idxs.append(idx)
    return jnp.concatenate(vals, axis=0), jnp.concatenate(idxs, axis=0)


def _topk_kernel(s_ref, i1_ref, i2_ref, gt_ref, i1_scr, i2_scr, gt_scr):
    tb = s_ref.shape[2]
    kk = PEER_TOPK
    iota_n = lax.broadcasted_iota(I32, (PEER_NKEYS, tb), 0).astype(F32)
    nrow = kk + (kk - 1) * 8
    r = lax.broadcasted_iota(I32, (nrow, tb), 0)
    a_i = jnp.where(r < kk, 0, 1 + ((r - kk) >> 3))
    b_i = jnp.where(r < kk, r, (r - kk) & 7)
    valid = (a_i + 1) * (b_i + 1) <= kk
    pos = (a_i * kk + b_i).astype(F32)

    def head(h, carry):
        v1, n1 = _top16(s_ref[2 * h], iota_n)
        v2, n2 = _top16(s_ref[2 * h + 1], iota_n)
        cand = [v1[0:1] + v2]
        eid = [n1[0:1] * float(PEER_NKEYS) + n2]
        for a in range(1, kk):
            cand.append(v1[a:a + 1] + v2[0:8])
            eid.append(n1[a:a + 1] * float(PEER_NKEYS) + n2[0:8])
        cand = jnp.where(valid, jnp.concatenate(cand, axis=0), -jnp.inf)
        eid = jnp.concatenate(eid, axis=0)
        tops, es = [], []
        for _ in range(kk):
            m = jnp.max(cand, axis=0, keepdims=True)
            p = jnp.min(jnp.where(cand == m, pos, 1e9), axis=0, keepdims=True)
            sel = pos == p
            es.append(jnp.max(jnp.where(sel, eid, -1.0), axis=0, keepdims=True))
            cand = jnp.where(sel, -jnp.inf, cand)
            tops.append(m)
        top = jnp.concatenate(tops, axis=0)
        e = jnp.concatenate(es, axis=0)
        ex = jnp.exp(top - top[0:1])
        gate = ex / jnp.sum(ex, axis=0, keepdims=True)
        e1 = jnp.floor(e * (1.0 / PEER_NKEYS))
        rows = pl.ds(pl.multiple_of(h * kk, kk), kk)
        i1_scr[rows, :] = e1
        i2_scr[rows, :] = e - e1 * float(PEER_NKEYS)
        gt_scr[rows, :] = gate
        return carry

    lax.fori_loop(0, PEER_HEADS, head, 0)
    i1_ref[...] = i1_scr[...].T.astype(I32)
    i2_ref[...] = i2_scr[...].T.astype(I32)
    gt_ref[...] = gt_scr[...].T


def _topk_call(st, tb):
    nhp, nk, r = st.shape
    slots = PEER_HEADS * PEER_TOPK
    out_spec = pl.BlockSpec((tb, slots), lambda i: (i, 0))
    return pl.pallas_call(
        _topk_kernel,
        grid=(r // tb,),
        in_specs=[pl.BlockSpec((nhp, nk, tb), lambda i: (0, 0, i))],
        out_specs=[out_spec, out_spec, out_spec],
        out_shape=[jax.ShapeDtypeStruct((r, slots), I32),
                   jax.ShapeDtypeStruct((r, slots), I32),
                   jax.ShapeDtypeStruct((r, slots), F32)],
        scratch_shapes=[pltpu.VMEM((slots, tb), F32)] * 3,
        name="topk",
        compiler_params=_params(("arbitrary",)),
    )(st)


def _peer_kernel(h2_ref, i1_ref, i2_ref, gt_ref, u_ref, v_ref, x1_ref, g2_ref, fg_ref, o_ref, w_scr, acc_scr):
    tb = h2_ref.shape[0]
    eb = u_ref.shape[0]
    e = pl.program_id(1)
    nk = PEER_NKEYS

    @pl.when(e == 0)
    def _build():
        iota_s = lax.broadcasted_iota(I32, (nk, nk), 0)

        def body(t8, carry):
            for rr in range(8):
                t = t8 * 8 + rr
                i1r = i1_ref[pl.ds(t, 1), :]
                i2r = i2_ref[pl.ds(t, 1), :]
                gtr = gt_ref[pl.ds(t, 1), :]
                g1t = jnp.where(iota_s == i1r, gtr, 0.0).astype(BF16)
                p2t = jnp.where(iota_s == i2r, 1.0, 0.0).astype(BF16)
                w_scr[pl.ds(pl.multiple_of(t * W_PITCH, 8), nk), :] = _dot_nt(g1t, p2t)
            return carry

        lax.fori_loop(0, tb // 8, body, 0)
        acc_scr[...] = jnp.zeros(acc_scr.shape, F32)

    a = _dot_nt(h2_ref[...], u_ref[...])
    pieces = []
    for k in range(eb // nk):
        n1 = e * (eb // nk) + k
        wn = w_scr[pl.ds(n1, tb, stride=W_PITCH), :]
        ak = a[:, k * nk:(k + 1) * nk]
        act = 0.5 * ak * (1.0 + lax.erf(ak * _SQRT_HALF))
        pieces.append((wn * act).astype(BF16))
    acc_scr[...] += _dot(jnp.concatenate(pieces, axis=1), v_ref[...])

    @pl.when(e == pl.num_programs(1) - 1)
    def _fin():
        x2 = x1_ref[...] + g2_ref[0] * acc_scr[...]
        o_ref[...] = _rms(x2) * fg_ref[...]


def _peer_call(h2, i1, i2, gt, u16, v16, x1, mod3, final_g, seq, tb, eb):
    r, d = h2.shape
    ne = u16.shape[0]
    slots = i1.shape[1]
    tok = lambda i, e: (i, 0)
    return pl.pallas_call(
        _peer_kernel,
        grid=(r // tb, ne // eb),
        in_specs=[pl.BlockSpec((tb, d), tok),
                  pl.BlockSpec((tb, slots), tok),
                  pl.BlockSpec((tb, slots), tok),
                  pl.BlockSpec((tb, slots), tok),
                  pl.BlockSpec((eb, d), lambda i, e: (e, 0)),
                  pl.BlockSpec((eb, d), lambda i, e: (e, 0)),
                  pl.BlockSpec((tb, d), tok, pipeline_mode=pl.Buffered(1)),
                  pl.BlockSpec((1, 1, d), lambda i, e: (5 + N_MOD * ((i * tb) // seq), 0, 0)),
                  pl.BlockSpec((1, d), lambda i, e: (0, 0))],
        out_specs=pl.BlockSpec((tb, d), tok),
        out_shape=jax.ShapeDtypeStruct((r, d), F32),
        scratch_shapes=[pltpu.VMEM((tb * W_PITCH, PEER_NKEYS), F32), pltpu.VMEM((tb, d), F32)],
        name="peer",
        compiler_params=_params(("arbitrary", "arbitrary")),
    )(h2, i1, i2, gt, u16, v16, x1, mod3, final_g)


def kernel(x, c, ctx, c_ctx, w_mod, b_mod, norm1_g, norm2_g, w_in, pool_w, pool_scale, hg_lb_logits, hg_norm_g,
           w_a, w_b, w_o, peer_wq, peer_keys, peer_u, peer_v, final_g):
    batch, seq, d = x.shape
    ctx_len = ctx.shape[1]
    assert w_mod.shape[0] == 1 and seq == GRID_W * GRID_W and hg_lb_logits.shape[1] == 2
    hw = HG_HEADS * GROUP

    cvec = jnp.concatenate([c, c_ctx[None, :], jnp.zeros((8 - batch - 1, d), F32)], axis=0)
    mod = _mod_call(cvec, w_mod[0], b_mod[0][None, :])
    mod3 = mod.reshape(8 * N_MOD, 1, d)

    w_in16 = w_in[0].astype(BF16)
    g1n = norm1_g[0][None, :]
    x2d = x.reshape(batch * seq, d)
    z = _inproj_call(x2d, mod3, 0, 1, seq, g1n, w_in16, 0, w_in16.shape[1] // 1024, 1024, 1024)
    zc = _inproj_call(ctx.reshape(batch * ctx_len, d), mod3, N_MOD * batch, N_MOD * batch + 1, None, g1n,
                      w_in16, 1, 2, 256, 1024)

    lbl = hg_lb_logits.reshape(2 * hg_lb_logits.shape[1], hw)
    sf, sb = _hgrn_ctx_call(zc, lbl, batch, ctx_len)
    yhg = _hgrn_call(z, lbl, hg_norm_g[0][None, :], sf, sb, batch, seq, 2)
    ypool = _pool_call(z, pool_w[0], pool_scale[0][None, :], batch, seq)

    nhp = 2 * PEER_HEADS
    x1, h2, st = _merge_call(x2d, ypool, yhg, z, mod3, norm2_g[0][None, :],
                             w_a[0].astype(BF16), w_b[0].astype(BF16), w_o[0].astype(BF16),
                             peer_wq[0].T.astype(BF16),
                             peer_keys[0].reshape(nhp, PEER_NKEYS, GROUP).astype(BF16), seq, 256)
    i1, i2, gt = _topk_call(st, 128)
    out = _peer_call(h2, i1, i2, gt, peer_u[0].astype(BF16), peer_v[0].astype(BF16), x1, mod3,
                     final_g[None, :], seq, 512, 512)
    return out.reshape(batch, seq, d)
```

```python
import functools
import math

import jax
import jax.numpy as jnp
from jax import lax
from jax.experimental import pallas as pl
from jax.experimental.pallas import tpu as pltpu

F32 = jnp.float32
BF16 = jnp.bfloat16
I32 = jnp.int32

EPS = 1e-6
GRID_W = 64
POOL_WINDOWS = (2, 4, 8, 16)
POOL_PAD = 8
GROUP = 128
HG_HEADS = 4
HG_CHUNK = 16
HG_BLOCK = 128
HG_MID = 8
PEER_HEADS = 8
PEER_NKEYS = 128
PEER_TOPK = 16
N_MOD = 6
W_PITCH = 136
VMEM_LIMIT = 60 * 1024 * 1024

_NT = (((1,), (1,)), ((), ()))
_SQRT_HALF = math.sqrt(0.5)


def _dot(a, b):
    return jnp.dot(a, b, preferred_element_type=F32)


def _dot_nt(a, b):
    return lax.dot_general(a, b, _NT, preferred_element_type=F32)


def _hi_lo(a):
    hi = a.astype(BF16)
    return hi, (a - hi.astype(F32)).astype(BF16)


def _sigmoid(x):
    return 1.0 / (1.0 + jnp.exp(-x))


def _rms(x):
    return x * lax.rsqrt(jnp.mean(x * x, axis=-1, keepdims=True) + EPS)


def _params(sem):
    return pltpu.CompilerParams(dimension_semantics=sem, vmem_limit_bytes=VMEM_LIMIT)


def _mod_kernel(c_ref, w_ref, b_ref, o_ref):
    c = c_ref[...]
    s = c * _sigmoid(c)
    s_hi, s_lo = _hi_lo(s)
    w_hi, w_lo = _hi_lo(w_ref[...])
    o_ref[...] = _dot(s_hi, w_hi) + (_dot(s_hi, w_lo) + _dot(s_lo, w_hi)) + b_ref[...]


def _mod_call(cvec, w_mod, b_mod):
    rows, d = cvec.shape
    n = w_mod.shape[1]
    tn = n // 4
    return pl.pallas_call(
        _mod_kernel,
        grid=(n // tn,),
        in_specs=[pl.BlockSpec((rows, d), lambda j: (0, 0)),
                  pl.BlockSpec((d, tn), lambda j: (0, j)),
                  pl.BlockSpec((1, tn), lambda j: (0, j))],
        out_specs=pl.BlockSpec((rows, tn), lambda j: (0, j)),
        out_shape=jax.ShapeDtypeStruct((rows, n), F32),
        name="mod",
        compiler_params=_params(("arbitrary",)),
    )(cvec, w_mod, b_mod)


def _inproj_kernel(x_ref, sh_ref, sc_ref, g_ref, w_ref, o_ref, h_scr):
    @pl.when(pl.program_id(1) == 0)
    def _():
        h = _rms(x_ref[...]) * g_ref[...] * (1.0 + sc_ref[0]) + sh_ref[0]
        h_scr[...] = h.astype(BF16)

    o_ref[...] = _dot(h_scr[...], w_ref[...])


def _inproj_call(x2d, mod3, sh_row, sc_row, rows_per_mod, g, w16, col0, ncols, tm, tn):
    r, d = x2d.shape
    if rows_per_mod is None:
        sh_map = lambda i, j: (sh_row, 0, 0)
        sc_map = lambda i, j: (sc_row, 0, 0)
    else:
        sh_map = lambda i, j: (sh_row + N_MOD * ((i * tm) // rows_per_mod), 0, 0)
        sc_map = lambda i, j: (sc_row + N_MOD * ((i * tm) // rows_per_mod), 0, 0)
    return pl.pallas_call(
        _inproj_kernel,
        grid=(r // tm, ncols),
        in_specs=[pl.BlockSpec((tm, d), lambda i, j: (i, 0)),
                  pl.BlockSpec((1, 1, d), sh_map),
                  pl.BlockSpec((1, 1, d), sc_map),
                  pl.BlockSpec((1, d), lambda i, j: (0, 0)),
                  pl.BlockSpec((d, tn), lambda i, j: (0, j + col0))],
        out_specs=pl.BlockSpec((tm, tn), lambda i, j: (i, j)),
        out_shape=jax.ShapeDtypeStruct((r, ncols * tn), F32),
        scratch_shapes=[pltpu.VMEM((tm, d), BF16)],
        name="inproj",
        compiler_params=_params(("arbitrary", "arbitrary")),
    )(x2d, mod3, mod3, g, w16)


def _pool_kernel(p_ref, w_ref, sc_ref, o_ref, pad_scr, tmp_scr):
    gw, pp = GRID_W, POOL_PAD
    pad_scr[...] = jnp.zeros(pad_scr.shape, F32)
    r_i = lax.broadcasted_iota(I32, (gw, gw, GROUP), 0)
    c_i = lax.broadcasted_iota(I32, (gw, gw, GROUP), 1)
    for g, win in enumerate(POOL_WINDOWS):
        half = win // 2
        pg = p_ref[:, g * GROUP:(g + 1) * GROUP].reshape(gw, gw, GROUP)
        pad_scr[pp:pp + gw, pp:pp + gw, :] = pg
        acc = None
        for k in range(-half, win - half):
            sl = pad_scr[pp + k:pp + k + gw, :, :]
            acc = sl if acc is None else acc + sl
        tmp_scr[...] = acc
        acc = None
        for k in range(-half, win - half):
            sl = tmp_scr[:, pp + k:pp + k + gw, :]
            acc = sl if acc is None else acc + sl
        cr = jnp.minimum(r_i - half + win, gw) - jnp.maximum(r_i - half, 0)
        cc = jnp.minimum(c_i - half + win, gw) - jnp.maximum(c_i - half, 0)
        mean = acc / (cr * cc).astype(F32)
        d = (mean - pg).reshape(gw * gw, GROUP)
        y = _dot(d.astype(BF16), w_ref[g].astype(BF16)) * sc_ref[:, g * GROUP:(g + 1) * GROUP]
        o_ref[:, g * GROUP:(g + 1) * GROUP] = y.astype(BF16)


def _pool_call(z, pool_w, pool_scale, batch, seq):
    width = len(POOL_WINDOWS) * GROUP
    ext = GRID_W + 2 * POOL_PAD
    return pl.pallas_call(
        _pool_kernel,
        grid=(batch,),
        in_specs=[pl.BlockSpec((seq, width), lambda b: (b, 0)),
                  pl.BlockSpec(pool_w.shape, lambda b: (0, 0, 0)),
                  pl.BlockSpec((1, width), lambda b: (0, 0))],
        out_specs=pl.BlockSpec((seq, width), lambda b: (b, 0)),
        out_shape=jax.ShapeDtypeStruct((batch * seq, width), BF16),
        scratch_shapes=[pltpu.VMEM((ext, ext, GROUP), F32), pltpu.VMEM((GRID_W, ext, GROUP), F32)],
        name="pool",
        compiler_params=_params(("arbitrary",)),
    )(z, pool_w, pool_scale)


def _hgrn_masks(rev):
    t, c = HG_BLOCK, HG_CHUNK
    row = lax.broadcasted_iota(I32, (t, t), 0)
    col = lax.broadcasted_iota(I32, (t, t), 1)
    sh = c.bit_length() - 1
    same = (row >> sh) == (col >> sh)
    tri = (col >= row) if rev else (col <= row)
    mid = ((row >> sh) << sh) + HG_MID
    to_mid = (col >= mid) if rev else (col <= mid)
    m_all = jnp.concatenate([jnp.where(same & tri, 1.0, 0.0),
                             jnp.where(same, 1.0, 0.0),
                             jnp.where(same & to_mid, 1.0, 0.0)], axis=0).astype(BF16)
    return m_all, same & tri, col >> sh


def _hgrn_blocks(chains, masks, emit_out):
    t, c = HG_BLOCK, HG_CHUNK
    nch = t // c
    nc = len(chains)
    revs = [ch[0] for ch in chains]
    qs = [ch[2] for ch in chains]
    vs = [ch[3] for ch in chains]
    sts = [ch[5] for ch in chains]
    ks, bs = [], []
    for rev, x_f, _, _, lb, _ in chains:
        f = lb + (1.0 - lb) * _sigmoid(x_f)
        g_hi, g_lo = _hi_lo(jnp.log(f))
        m_all = masks[rev][0]
        ks.append(1.0 - f)
        bs.append(_dot(m_all, g_hi) + _dot(m_all, g_lo))
    b_loc = [b[0:t] for b in bs]
    b_tot = [b[t:2 * t] for b in bs]
    b_mid = [b[2 * t:3 * t] for b in bs]
    kd16 = [(ks[i] * jnp.exp(b_tot[i] - b_loc[i])).astype(BF16) for i in range(nc)]
    vt = [v.T for v in vs]
    a_all = [jnp.exp(b) for b in b_tot]
    os_ = [None] * nc
    if emit_out:
        a16 = [(qs[i] * jnp.exp(b_loc[i])).astype(BF16) for i in range(nc)]
        qd16 = [(qs[i] * jnp.exp(b_loc[i] - b_mid[i])).astype(BF16) for i in range(nc)]
        kg16 = [(ks[i] * jnp.exp(b_mid[i] - b_loc[i])).astype(BF16) for i in range(nc)]
        attn = [jnp.where(masks[revs[i]][1], _dot_nt(qd16[i], kg16[i]), 0.0) for i in range(nc)]
        os_ = [_dot(attn[i].astype(BF16), vs[i].astype(BF16)) for i in range(nc)]
        o_inter = [[None] * nch for _ in range(nc)]
    for step in range(nch):
        for i in range(nc):
            n = nch - 1 - step if revs[i] else step
            if emit_out:
                o_inter[i][n] = _dot_nt(a16[i][n * c:(n + 1) * c], sts[i].astype(BF16))
            ut = _dot(jnp.where(masks[revs[i]][2] == n, vt[i], 0.0).astype(BF16), kd16[i])
            sts[i] = sts[i] * a_all[i][n * c:n * c + 1, :] + ut
    if emit_out:
        os_ = [os_[i] + jnp.concatenate(o_inter[i], axis=0) for i in range(nc)]
    return list(zip(os_, sts))


def _lower_bounds(lbl_ref):
    out = []
    for d in range(2):
        l0, l1 = lbl_ref[2 * d:2 * d + 1, :], lbl_ref[2 * d + 1:2 * d + 2, :]
        m = jnp.maximum(l0, l1)
        e0, e1 = jnp.exp(l0 - m), jnp.exp(l1 - m)
        out.append(e0 / (e0 + e1))
    return out


def _hgrn_run(nheads, seq, lbl_ref, f_refs, q_ref, v_ref, st_scr, o_scrs):
    t = HG_BLOCK
    nblk = seq // t
    masks = (_hgrn_masks(False), _hgrn_masks(True))
    lbs = _lower_bounds(lbl_ref)
    emit_out = o_scrs is not None

    def body(i, carry):
        chains = []
        for d in range(2):
            blk = (nblk - 1 - i) if d else i
            rows = pl.ds(pl.multiple_of(blk * t, t), t)
            for h in range(nheads):
                ls = slice(h * GROUP, (h + 1) * GROUP)
                q = q_ref[rows, ls] if emit_out else None
                chains.append((d, h, rows, ls, f_refs[d][rows, ls], q, v_ref[rows, ls], st_scr[d * nheads + h]))
        results = _hgrn_blocks([(d, x_f, q, v, lbs[d][:, ls], st) for d, h, rows, ls, x_f, q, v, st in chains],
                               masks, emit_out)
        for (d, h, rows, ls, *_), (o, st) in zip(chains, results):
            st_scr[d * nheads + h] = st
            if emit_out:
                o_scrs[d][rows, ls] = o
        return carry

    lax.fori_loop(0, nblk, body, 0)


def _hgrn_ctx_kernel(v_ref, ff_ref, fb_ref, lbl_ref, sf_ref, sb_ref, st_scr, *, seq):
    nheads = sf_ref.shape[1]
    st_scr[...] = jnp.zeros(st_scr.shape, F32)
    _hgrn_run(nheads, seq, lbl_ref, (ff_ref, fb_ref), None, v_ref, st_scr, None)
    sf_ref[0] = st_scr[0:nheads]
    sb_ref[0] = st_scr[nheads:2 * nheads]


def _hgrn_kernel(q_ref, v_ref, ff_ref, fb_ref, og_ref, lbl_ref, ng_ref, sf_ref, sb_ref, y_ref,
                 st_scr, of_scr, ob_scr, *, seq):
    nheads = sf_ref.shape[1]
    st_scr[0:nheads] = sf_ref[0]
    st_scr[nheads:2 * nheads] = sb_ref[0]
    _hgrn_run(nheads, seq, lbl_ref, (ff_ref, fb_ref), q_ref, v_ref, st_scr, (of_scr, ob_scr))
    rb = 4 * HG_BLOCK

    def epilogue(i, carry):
        rows = pl.ds(pl.multiple_of(i * rb, rb), rb)
        for h in range(nheads):
            ls = slice(h * GROUP, (h + 1) * GROUP)
            og = og_ref[rows, ls]
            y = _rms(of_scr[rows, ls] + ob_scr[rows, ls]) * ng_ref[...] * (og * _sigmoid(og))
            y_ref[rows, ls] = y.astype(BF16)
        return carry

    lax.fori_loop(0, seq // rb, epilogue, 0)


def _hgrn_ctx_call(zc, lbl, batch, seq):
    hw = HG_HEADS * GROUP
    st_spec = pl.BlockSpec((1, HG_HEADS, GROUP, GROUP), lambda b: (b, 0, 0, 0))
    st_shape = jax.ShapeDtypeStruct((batch, HG_HEADS, GROUP, GROUP), F32)
    return pl.pallas_call(
        functools.partial(_hgrn_ctx_kernel, seq=seq),
        grid=(batch,),
        in_specs=[pl.BlockSpec((seq, hw), lambda b: (b, 0)),
                  pl.BlockSpec((seq, hw), lambda b: (b, 1)),
                  pl.BlockSpec((seq, hw), lambda b: (b, 2)),
                  pl.BlockSpec((4, hw), lambda b: (0, 0))],
        out_specs=[st_spec, st_spec],
        out_shape=[st_shape, st_shape],
        scratch_shapes=[pltpu.VMEM((2 * HG_HEADS, GROUP, GROUP), F32)],
        name="hgrn_ctx",
        compiler_params=_params(("arbitrary",)),
    )(zc, zc, zc, lbl)


def _hgrn_call(z, lbl, norm_g, sf, sb, batch, seq, nheads):
    ngrp = HG_HEADS // nheads
    w = nheads * GROUP
    col = lambda k: (lambda b, h: (b, k * ngrp + h))
    big = lambda k: pl.BlockSpec((seq, w), col(k), pipeline_mode=pl.Buffered(1))
    st_spec = pl.BlockSpec((1, nheads, GROUP, GROUP), lambda b, h: (b, h, 0, 0))
    return pl.pallas_call(
        functools.partial(_hgrn_kernel, seq=seq),
        grid=(batch, ngrp),
        in_specs=[big(1), big(2), big(3), big(4), big(5),
                  pl.BlockSpec((4, w), lambda b, h: (0, h)),
                  pl.BlockSpec((1, GROUP), lambda b, h: (0, 0)),
                  st_spec, st_spec],
        out_specs=pl.BlockSpec((seq, w), lambda b, h: (b, h)),
        out_shape=jax.ShapeDtypeStruct((batch * seq, HG_HEADS * GROUP), BF16),
        scratch_shapes=[pltpu.VMEM((2 * nheads, GROUP, GROUP), F32),
                        pltpu.VMEM((seq, w), F32), pltpu.VMEM((seq, w), F32)],
        name="hgrn",
        compiler_params=_params(("arbitrary", "arbitrary")),
    )(z, z, z, z, z, lbl, norm_g, sf, sb)


def _merge_kernel(x_ref, yp_ref, yh_ref, gp_ref, gh_ref, g1_ref, sh_ref, sc_ref, ng_ref,
                  wa_ref, wb_ref, wo_ref, wqt_ref, keys_ref, x1_ref, h2_ref, st_ref):
    m = (_sigmoid(gp_ref[...]) * _dot(yp_ref[...], wa_ref[...])
         + _sigmoid(gh_ref[...]) * _dot(yh_ref[...], wb_ref[...]))
    x1 = x_ref[...] + g1_ref[0] * _dot(m.astype(BF16), wo_ref[...])
    x1_ref[...] = x1
    h2 = _rms(x1) * ng_ref[...] * (1.0 + sc_ref[0]) + sh_ref[0]
    h2_ref[...] = h2.astype(BF16)
    qt = _dot(wqt_ref[...], h2.T.astype(BF16))
    for hp in range(2 * PEER_HEADS):
        st_ref[hp] = _dot(keys_ref[hp], qt[hp * GROUP:(hp + 1) * GROUP].astype(BF16))


def _merge_call(x2d, ypool, yhg, z, mod3, norm2_g, wa16, wb16, wo16, wqt16, keys16, seq, tm):
    r, d = x2d.shape
    gcol0 = (z.shape[1] - 2 * d) // d
    modrow = lambda k: (lambda i: (k + N_MOD * ((i * tm) // seq), 0, 0))
    const2 = lambda i: (0, 0)
    nhp = 2 * PEER_HEADS
    return pl.pallas_call(
        _merge_kernel,
        grid=(r // tm,),
        in_specs=[pl.BlockSpec((tm, d), lambda i: (i, 0)),
                  pl.BlockSpec((tm, ypool.shape[1]), lambda i: (i, 0)),
                  pl.BlockSpec((tm, yhg.shape[1]), lambda i: (i, 0)),
                  pl.BlockSpec((tm, d), lambda i: (i, gcol0)),
                  pl.BlockSpec((tm, d), lambda i: (i, gcol0 + 1)),
                  pl.BlockSpec((1, 1, d), modrow(2)),
                  pl.BlockSpec((1, 1, d), modrow(3)),
                  pl.BlockSpec((1, 1, d), modrow(4)),
                  pl.BlockSpec((1, d), const2),
                  pl.BlockSpec(wa16.shape, const2),
                  pl.BlockSpec(wb16.shape, const2),
                  pl.BlockSpec(wo16.shape, const2),
                  pl.BlockSpec(wqt16.shape, const2),
                  pl.BlockSpec(keys16.shape, lambda i: (0, 0, 0))],
        out_specs=[pl.BlockSpec((tm, d), lambda i: (i, 0)),
                   pl.BlockSpec((tm, d), lambda i: (i, 0)),
                   pl.BlockSpec((nhp, PEER_NKEYS, tm), lambda i: (0, 0, i))],
        out_shape=[jax.ShapeDtypeStruct((r, d), F32),
                   jax.ShapeDtypeStruct((r, d), BF16),
                   jax.ShapeDtypeStruct((nhp, PEER_NKEYS, r), F32)],
        name="merge",
        compiler_params=_params(("arbitrary",)),
    )(x2d, ypool, yhg, z, z, mod3, mod3, mod3, norm2_g, wa16, wb16, wo16, wqt16, keys16)


def _top16_pair(va, vb, iota_n):
    vals, idxs = ([], []), ([], [])
    vs = [va, vb]
    for _ in range(PEER_TOPK):
        for s in range(2):
            v = vs[s]
            m = jnp.max(v, axis=0, keepdims=True)
            idx = jnp.min(jnp.where(v == m, iota_n, float(PEER_NKEYS)), axis=0, keepdims=True)
            vs[s] = jnp.where(iota_n == idx, -jnp.inf, v)
            vals[s].append(m)
            idxs[s].append(idx)
    cat = lambda xs: jnp.concatenate(xs, axis=0)
    return cat(vals[0]), cat(idxs[0]), cat(vals[1]), cat(idxs[1])


def _topk_unit(s1, s2):
    tg = s1.shape[1]
    kk = PEER_TOPK
    iota_n = lax.broadcasted_iota(I32, (PEER_NKEYS, tg), 0).astype(F32)
    nrow = kk + (kk - 1) * 8
    r = lax.broadcasted_iota(I32, (nrow, tg), 0)
    a_i = jnp.where(r < kk, 0, 1 + ((r - kk) >> 3))
    b_i = jnp.where(r < kk, r, (r - kk) & 7)
    valid = (a_i + 1) * (b_i + 1) <= kk
    pos = (a_i * kk + b_i).astype(F32)

    v1, n1, v2, n2 = _top16_pair(s1, s2, iota_n)
    cand = [v1[0:1] + v2]
    eid = [n1[0:1] * float(PEER_NKEYS) + n2]
    for a in range(1, kk):
        cand.append(v1[a:a + 1] + v2[0:8])
        eid.append(n1[a:a + 1] * float(PEER_NKEYS) + n2[0:8])
    cand = jnp.where(valid, jnp.concatenate(cand, axis=0), -jnp.inf)
    eid = jnp.concatenate(eid, axis=0)
    tops, es = [], []
    for _ in range(kk):
        m = jnp.max(cand, axis=0, keepdims=True)
        p = jnp.min(jnp.where(cand == m, pos, 1e9), axis=0, keepdims=True)
        sel = pos == p
        es.append(jnp.max(jnp.where(sel, eid, -1.0), axis=0, keepdims=True))
        cand = jnp.where(sel, -jnp.inf, cand)
        tops.append(m)
    top = jnp.concatenate(tops, axis=0)
    e = jnp.concatenate(es, axis=0)
    ex = jnp.exp(top - top[0:1])
    gate = ex / jnp.sum(ex, axis=0, keepdims=True)
    e1 = jnp.floor(e * (1.0 / PEER_NKEYS))
    return e1, e - e1 * float(PEER_NKEYS), gate


def _peer_kernel(s_ref, h2_ref, ut_ref, v_ref, x1_ref, g2_ref, fg_ref, o_ref, w_scr, acc_scr, rt_scr, rr_scr):
    tb, d = h2_ref.shape
    eb = v_ref.shape[0]
    i = pl.program_id(0)
    e = pl.program_id(1)
    nk = PEER_NKEYS
    ngrp = tb // GROUP
    unroll = 32

    @pl.when((i == 0) & (e == 0))
    def _init():
        rt_scr[...] = jnp.zeros(rt_scr.shape, F32)

    @pl.when(e == 0)
    def _build():
        rd = lax.rem(i + 1, 2)
        for fld in range(3):
            for g in range(ngrp):
                rr_scr[fld, g * GROUP:(g + 1) * GROUP, :] = rt_scr[rd, fld, g].T
        iota_s = lax.broadcasted_iota(I32, (nk, nk), 0).astype(F32)

        def body(tt, carry):
            ts = [tt * unroll + r for r in range(unroll)]
            g1t = [jnp.where(iota_s == rr_scr[0, pl.ds(t, 1), :], rr_scr[2, pl.ds(t, 1), :], 0.0).astype(BF16)
                   for t in ts]
            p2t = [jnp.where(iota_s == rr_scr[1, pl.ds(t, 1), :], 1.0, 0.0).astype(BF16) for t in ts]
            ws = [_dot_nt(g, p) for g, p in zip(g1t, p2t)]
            for t, w in zip(ts, ws):
                w_scr[pl.ds(pl.multiple_of(t * W_PITCH, 8), nk), :] = w
            return carry

        lax.fori_loop(0, tb // unroll, body, 0)
        acc_scr[...] = jnp.zeros(acc_scr.shape, F32)

    head = lax.rem(e, PEER_HEADS)
    grp = e // PEER_HEADS
    i1, i2, gate = _topk_unit(s_ref[0], s_ref[1])
    wr = lax.rem(i, 2)
    rows = pl.ds(pl.multiple_of(head * PEER_TOPK, PEER_TOPK), PEER_TOPK)
    rt_scr[wr, 0, grp, rows, :] = i1
    rt_scr[wr, 1, grp, rows, :] = i2
    rt_scr[wr, 2, grp, rows, :] = gate

    n1_0 = e * (eb // nk)
    pieces = []
    for k2 in range(eb // (2 * nk)):
        a2 = _dot(h2_ref[...], ut_ref[:, 2 * k2 * nk:2 * (k2 + 1) * nk])
        for k in (2 * k2, 2 * k2 + 1):
            ak = a2[:, (k - 2 * k2) * nk:(k - 2 * k2 + 1) * nk]
            wn = w_scr[pl.ds(n1_0 + k, tb, stride=W_PITCH), :]
            act = 0.5 * ak * (1.0 + lax.erf(ak * _SQRT_HALF))
            pieces.append((wn * act).astype(BF16))
    wg = jnp.concatenate(pieces, axis=1)
    tn = 256
    for j in range(d // tn):
        acc_scr[:, j * tn:(j + 1) * tn] += _dot(wg, v_ref[:, j * tn:(j + 1) * tn])

    @pl.when(e == pl.num_programs(1) - 1)
    def _fin():
        x2 = x1_ref[...] + g2_ref[0] * acc_scr[...]
        o_ref[...] = _rms(x2) * fg_ref[...]


def _peer_call(st, h2, ut16, v16, x1, mod3, final_g, seq, tb, eb):
    r, d = h2.shape
    ne = v16.shape[0]
    nblk = r // tb
    ngrp = tb // GROUP
    nsteps = ne // eb
    assert nsteps == PEER_HEADS * ngrp, "one (head, token group) routing unit per expert step"
    slots = PEER_HEADS * PEER_TOPK
    prev = lambda i: jnp.maximum(i - 1, 0)
    tok = lambda i, e: (prev(i), 0)
    return pl.pallas_call(
        _peer_kernel,
        grid=(nblk + 1, nsteps),
        in_specs=[pl.BlockSpec((2, PEER_NKEYS, GROUP),
                               lambda i, e: (lax.rem(e, PEER_HEADS), 0,
                                             jnp.minimum(i, nblk - 1) * ngrp + e // PEER_HEADS)),
                  pl.BlockSpec((tb, d), tok),
                  pl.BlockSpec((d, eb), lambda i, e: (0, e)),
                  pl.BlockSpec((eb, d), lambda i, e: (e, 0)),
                  pl.BlockSpec((tb, d), tok, pipeline_mode=pl.Buffered(1)),
                  pl.BlockSpec((1, 1, d), lambda i, e: (5 + N_MOD * ((prev(i) * tb) // seq), 0, 0)),
                  pl.BlockSpec((1, d), lambda i, e: (0, 0))],
        out_specs=pl.BlockSpec((tb, d), tok),
        out_shape=jax.ShapeDtypeStruct((r, d), F32),
        scratch_shapes=[pltpu.VMEM((tb * W_PITCH, PEER_NKEYS), F32),
                        pltpu.VMEM((tb, d), F32),
                        pltpu.VMEM((2, 3, ngrp, slots, GROUP), F32),
                        pltpu.VMEM((3, tb, slots), F32)],
        name="peer",
        compiler_params=_params(("arbitrary", "arbitrary")),
    )(st, h2, ut16, v16, x1, mod3, final_g)


def kernel(x, c, ctx, c_ctx, w_mod, b_mod, norm1_g, norm2_g, w_in, pool_w, pool_scale, hg_lb_logits, hg_norm_g,
           w_a, w_b, w_o, peer_wq, peer_keys, peer_u, peer_v, final_g):
    batch, seq, d = x.shape
    ctx_len = ctx.shape[1]
    assert w_mod.shape[0] == 1 and seq == GRID_W * GRID_W and hg_lb_logits.shape[1] == 2
    hw = HG_HEADS * GROUP

    cvec = jnp.concatenate([c, c_ctx[None, :], jnp.zeros((8 - batch - 1, d), F32)], axis=0)
    mod = _mod_call(cvec, w_mod[0], b_mod[0][None, :])
    mod3 = mod.reshape(8 * N_MOD, 1, d)

    w_in16 = w_in[0].astype(BF16)
    g1n = norm1_g[0][None, :]
    x2d = x.reshape(batch * seq, d)
    z = _inproj_call(x2d, mod3, 0, 1, seq, g1n, w_in16, 0, w_in16.shape[1] // 1024, 1024, 1024)
    zc = _inproj_call(ctx.reshape(batch * ctx_len, d), mod3, N_MOD * batch, N_MOD * batch + 1, None, g1n,
                      w_in16, 1, 2, 256, 1024)

    lbl = hg_lb_logits.reshape(2 * hg_lb_logits.shape[1], hw)
    sf, sb = _hgrn_ctx_call(zc, lbl, batch, ctx_len)
    yhg = _hgrn_call(z, lbl, hg_norm_g[0][None, :], sf, sb, batch, seq, 2)
    ypool = _pool_call(z, pool_w[0], pool_scale[0][None, :], batch, seq)

    nhp = 2 * PEER_HEADS
    x1, h2, st = _merge_call(x2d, ypool, yhg, z, mod3, norm2_g[0][None, :],
                             w_a[0].astype(BF16), w_b[0].astype(BF16), w_o[0].astype(BF16),
                             peer_wq[0].T.astype(BF16),
                             peer_keys[0].reshape(nhp, PEER_NKEYS, GROUP).astype(BF16), seq, 256)
    out = _peer_call(st, h2, peer_u[0].T.astype(BF16), peer_v[0].astype(BF16), x1, mod3,
                     final_g[None, :], seq, 512, 512)
    return out.reshape(batch, seq, d)
```

```python
import functools
import math

import jax
import jax.numpy as jnp
from jax import lax
from jax.experimental import pallas as pl
from jax.experimental.pallas import tpu as pltpu

F32 = jnp.float32
BF16 = jnp.bfloat16
I32 = jnp.int32

EPS = 1e-6
GRID_W = 64
POOL_WINDOWS = (2, 4, 8, 16)
POOL_PAD = 8
GROUP = 128
HG_HEADS = 4
HG_CHUNK = 16
HG_BLOCK = 128
HG_MID = 8
PEER_HEADS = 8
PEER_NKEYS = 128
PEER_TOPK = 16
N_MOD = 6
VMEM_LIMIT = 60 * 1024 * 1024

_NT = (((1,), (1,)), ((), ()))
_SQRT_HALF = math.sqrt(0.5)


def _dot(a, b):
    return jnp.dot(a, b, preferred_element_type=F32)


def _dot_nt(a, b):
    return lax.dot_general(a, b, _NT, preferred_element_type=F32)


def _hi_lo(a):
    hi = a.astype(BF16)
    return hi, (a - hi.astype(F32)).astype(BF16)


def _sigmoid(x):
    return 1.0 / (1.0 + jnp.exp(-x))


def _rms(x):
    return x * lax.rsqrt(jnp.mean(x * x, axis=-1, keepdims=True) + EPS)


def _params(sem):
    return pltpu.CompilerParams(dimension_semantics=sem, vmem_limit_bytes=VMEM_LIMIT)


def _mod_kernel(c_ref, w_ref, b_ref, o_ref):
    c = c_ref[...]
    s = c * _sigmoid(c)
    s_hi, s_lo = _hi_lo(s)
    w_hi, w_lo = _hi_lo(w_ref[...])
    o_ref[...] = _dot(s_hi, w_hi) + (_dot(s_hi, w_lo) + _dot(s_lo, w_hi)) + b_ref[...]


def _mod_call(cvec, w_mod, b_mod):
    rows, d = cvec.shape
    n = w_mod.shape[1]
    tn = n // 4
    return pl.pallas_call(
        _mod_kernel,
        grid=(n // tn,),
        in_specs=[pl.BlockSpec((rows, d), lambda j: (0, 0)),
                  pl.BlockSpec((d, tn), lambda j: (0, j)),
                  pl.BlockSpec((1, tn), lambda j: (0, j))],
        out_specs=pl.BlockSpec((rows, tn), lambda j: (0, j)),
        out_shape=jax.ShapeDtypeStruct((rows, n), F32),
        name="mod",
        compiler_params=_params(("arbitrary",)),
    )(cvec, w_mod, b_mod)


def _inproj_kernel(x_ref, sh_ref, sc_ref, g_ref, w_ref, o_ref, h_scr):
    @pl.when(pl.program_id(1) == 0)
    def _():
        h = _rms(x_ref[...]) * g_ref[...] * (1.0 + sc_ref[0]) + sh_ref[0]
        h_scr[...] = h.astype(BF16)

    o_ref[...] = _dot(h_scr[...], w_ref[...])


def _inproj_call(x2d, mod3, sh_row, sc_row, rows_per_mod, g, w16, col0, ncols, tm, tn):
    r, d = x2d.shape
    if rows_per_mod is None:
        sh_map = lambda i, j: (sh_row, 0, 0)
        sc_map = lambda i, j: (sc_row, 0, 0)
    else:
        sh_map = lambda i, j: (sh_row + N_MOD * ((i * tm) // rows_per_mod), 0, 0)
        sc_map = lambda i, j: (sc_row + N_MOD * ((i * tm) // rows_per_mod), 0, 0)
    return pl.pallas_call(
        _inproj_kernel,
        grid=(r // tm, ncols),
        in_specs=[pl.BlockSpec((tm, d), lambda i, j: (i, 0)),
                  pl.BlockSpec((1, 1, d), sh_map),
                  pl.BlockSpec((1, 1, d), sc_map),
                  pl.BlockSpec((1, d), lambda i, j: (0, 0)),
                  pl.BlockSpec((d, tn), lambda i, j: (0, j + col0))],
        out_specs=pl.BlockSpec((tm, tn), lambda i, j: (i, j)),
        out_shape=jax.ShapeDtypeStruct((r, ncols * tn), F32),
        scratch_shapes=[pltpu.VMEM((tm, d), BF16)],
        name="inproj",
        compiler_params=_params(("arbitrary", "arbitrary")),
    )(x2d, mod3, mod3, g, w16)


def _pool_kernel(p_ref, w_ref, sc_ref, o_ref, pad_scr, tmp_scr):
    gw, pp = GRID_W, POOL_PAD
    pad_scr[...] = jnp.zeros(pad_scr.shape, F32)
    r_i = lax.broadcasted_iota(I32, (gw, gw, GROUP), 0)
    c_i = lax.broadcasted_iota(I32, (gw, gw, GROUP), 1)
    for g, win in enumerate(POOL_WINDOWS):
        half = win // 2
        pg = p_ref[:, g * GROUP:(g + 1) * GROUP].reshape(gw, gw, GROUP)
        pad_scr[pp:pp + gw, pp:pp + gw, :] = pg
        acc = None
        for k in range(-half, win - half):
            sl = pad_scr[pp + k:pp + k + gw, :, :]
            acc = sl if acc is None else acc + sl
        tmp_scr[...] = acc
        acc = None
        for k in range(-half, win - half):
            sl = tmp_scr[:, pp + k:pp + k + gw, :]
            acc = sl if acc is None else acc + sl
        cr = jnp.minimum(r_i - half + win, gw) - jnp.maximum(r_i - half, 0)
        cc = jnp.minimum(c_i - half + win, gw) - jnp.maximum(c_i - half, 0)
        mean = acc / (cr * cc).astype(F32)
        d = (mean - pg).reshape(gw * gw, GROUP)
        y = _dot(d.astype(BF16), w_ref[g].astype(BF16)) * sc_ref[:, g * GROUP:(g + 1) * GROUP]
        o_ref[:, g * GROUP:(g + 1) * GROUP] = y.astype(BF16)


def _pool_call(z, pool_w, pool_scale, batch, seq):
    width = len(POOL_WINDOWS) * GROUP
    ext = GRID_W + 2 * POOL_PAD
    return pl.pallas_call(
        _pool_kernel,
        grid=(batch,),
        in_specs=[pl.BlockSpec((seq, width), lambda b: (b, 0)),
                  pl.BlockSpec(pool_w.shape, lambda b: (0, 0, 0)),
                  pl.BlockSpec((1, width), lambda b: (0, 0))],
        out_specs=pl.BlockSpec((seq, width), lambda b: (b, 0)),
        out_shape=jax.ShapeDtypeStruct((batch * seq, width), BF16),
        scratch_shapes=[pltpu.VMEM((ext, ext, GROUP), F32), pltpu.VMEM((GRID_W, ext, GROUP), F32)],
        name="pool",
        compiler_params=_params(("arbitrary",)),
    )(z, pool_w, pool_scale)


def _hgrn_masks(rev):
    t, c = HG_BLOCK, HG_CHUNK
    row = lax.broadcasted_iota(I32, (t, t), 0)
    col = lax.broadcasted_iota(I32, (t, t), 1)
    sh = c.bit_length() - 1
    same = (row >> sh) == (col >> sh)
    tri = (col >= row) if rev else (col <= row)
    mid = ((row >> sh) << sh) + HG_MID
    to_mid = (col >= mid) if rev else (col <= mid)
    m_all = jnp.concatenate([jnp.where(same & tri, 1.0, 0.0),
                             jnp.where(same, 1.0, 0.0),
                             jnp.where(same & to_mid, 1.0, 0.0)], axis=0).astype(BF16)
    return m_all, same & tri, col >> sh


def _hgrn_blocks(chains, masks, emit_out):
    t, c = HG_BLOCK, HG_CHUNK
    nch = t // c
    nc = len(chains)
    revs = [ch[0] for ch in chains]
    qs = [ch[2] for ch in chains]
    vs = [ch[3] for ch in chains]
    sts = [ch[5] for ch in chains]
    ks, bs = [], []
    for rev, x_f, _, _, lb, _ in chains:
        f = lb + (1.0 - lb) * _sigmoid(x_f)
        g_hi, g_lo = _hi_lo(jnp.log(f))
        m_all = masks[rev][0]
        ks.append(1.0 - f)
        bs.append(_dot(m_all, g_hi) + _dot(m_all, g_lo))
    b_loc = [b[0:t] for b in bs]
    b_tot = [b[t:2 * t] for b in bs]
    b_mid = [b[2 * t:3 * t] for b in bs]
    kd16 = [(ks[i] * jnp.exp(b_tot[i] - b_loc[i])).astype(BF16) for i in range(nc)]
    vt = [v.T for v in vs]
    a_all = [jnp.exp(b) for b in b_tot]
    os_ = [None] * nc
    if emit_out:
        a16 = [(qs[i] * jnp.exp(b_loc[i])).astype(BF16) for i in range(nc)]
        qd16 = [(qs[i] * jnp.exp(b_loc[i] - b_mid[i])).astype(BF16) for i in range(nc)]
        kg16 = [(ks[i] * jnp.exp(b_mid[i] - b_loc[i])).astype(BF16) for i in range(nc)]
        attn = [jnp.where(masks[revs[i]][1], _dot_nt(qd16[i], kg16[i]), 0.0) for i in range(nc)]
        os_ = [_dot(attn[i].astype(BF16), vs[i].astype(BF16)) for i in range(nc)]
        o_inter = [[None] * nch for _ in range(nc)]
    for step in range(nch):
        for i in range(nc):
            n = nch - 1 - step if revs[i] else step
            if emit_out:
                o_inter[i][n] = _dot_nt(a16[i][n * c:(n + 1) * c], sts[i].astype(BF16))
            ut = _dot(jnp.where(masks[revs[i]][2] == n, vt[i], 0.0).astype(BF16), kd16[i])
            sts[i] = sts[i] * a_all[i][n * c:n * c + 1, :] + ut
    if emit_out:
        os_ = [os_[i] + jnp.concatenate(o_inter[i], axis=0) for i in range(nc)]
    return list(zip(os_, sts))


def _lower_bounds(lbl_ref):
    out = []
    for d in range(2):
        l0, l1 = lbl_ref[2 * d:2 * d + 1, :], lbl_ref[2 * d + 1:2 * d + 2, :]
        m = jnp.maximum(l0, l1)
        e0, e1 = jnp.exp(l0 - m), jnp.exp(l1 - m)
        out.append(e0 / (e0 + e1))
    return out


def _hgrn_run(nheads, seq, lbl_ref, f_refs, q_ref, v_ref, st_scr, o_scrs):
    t = HG_BLOCK
    nblk = seq // t
    masks = (_hgrn_masks(False), _hgrn_masks(True))
    lbs = _lower_bounds(lbl_ref)
    emit_out = o_scrs is not None

    def body(i, carry):
        chains = []
        for d in range(2):
            blk = (nblk - 1 - i) if d else i
            rows = pl.ds(pl.multiple_of(blk * t, t), t)
            for h in range(nheads):
                ls = slice(h * GROUP, (h + 1) * GROUP)
                q = q_ref[rows, ls] if emit_out else None
                chains.append((d, h, rows, ls, f_refs[d][rows, ls], q, v_ref[rows, ls], st_scr[d * nheads + h]))
        results = _hgrn_blocks([(d, x_f, q, v, lbs[d][:, ls], st) for d, h, rows, ls, x_f, q, v, st in chains],
                               masks, emit_out)
        for (d, h, rows, ls, *_), (o, st) in zip(chains, results):
            st_scr[d * nheads + h] = st
            if emit_out:
                o_scrs[d][rows, ls] = o
        return carry

    lax.fori_loop(0, nblk, body, 0)


def _hgrn_ctx_kernel(v_ref, ff_ref, fb_ref, lbl_ref, sf_ref, sb_ref, st_scr, *, seq):
    nheads = sf_ref.shape[1]
    st_scr[...] = jnp.zeros(st_scr.shape, F32)
    _hgrn_run(nheads, seq, lbl_ref, (ff_ref, fb_ref), None, v_ref, st_scr, None)
    sf_ref[0] = st_scr[0:nheads]
    sb_ref[0] = st_scr[nheads:2 * nheads]


def _hgrn_kernel(q_ref, v_ref, ff_ref, fb_ref, og_ref, lbl_ref, ng_ref, sf_ref, sb_ref, y_ref,
                 st_scr, of_scr, ob_scr, *, seq):
    nheads = sf_ref.shape[1]
    st_scr[0:nheads] = sf_ref[0]
    st_scr[nheads:2 * nheads] = sb_ref[0]
    _hgrn_run(nheads, seq, lbl_ref, (ff_ref, fb_ref), q_ref, v_ref, st_scr, (of_scr, ob_scr))
    rb = 4 * HG_BLOCK

    def epilogue(i, carry):
        rows = pl.ds(pl.multiple_of(i * rb, rb), rb)
        for h in range(nheads):
            ls = slice(h * GROUP, (h + 1) * GROUP)
            og = og_ref[rows, ls]
            y = _rms(of_scr[rows, ls] + ob_scr[rows, ls]) * ng_ref[...] * (og * _sigmoid(og))
            y_ref[rows, ls] = y.astype(BF16)
        return carry

    lax.fori_loop(0, seq // rb, epilogue, 0)


def _hgrn_ctx_call(zc, lbl, batch, seq):
    hw = HG_HEADS * GROUP
    st_spec = pl.BlockSpec((1, HG_HEADS, GROUP, GROUP), lambda b: (b, 0, 0, 0))
    st_shape = jax.ShapeDtypeStruct((batch, HG_HEADS, GROUP, GROUP), F32)
    return pl.pallas_call(
        functools.partial(_hgrn_ctx_kernel, seq=seq),
        grid=(batch,),
        in_specs=[pl.BlockSpec((seq, hw), lambda b: (b, 0)),
                  pl.BlockSpec((seq, hw), lambda b: (b, 1)),
                  pl.BlockSpec((seq, hw), lambda b: (b, 2)),
                  pl.BlockSpec((4, hw), lambda b: (0, 0))],
        out_specs=[st_spec, st_spec],
        out_shape=[st_shape, st_shape],
        scratch_shapes=[pltpu.VMEM((2 * HG_HEADS, GROUP, GROUP), F32)],
        name="hgrn_ctx",
        compiler_params=_params(("arbitrary",)),
    )(zc, zc, zc, lbl)


def _hgrn_call(z, lbl, norm_g, sf, sb, batch, seq, nheads):
    ngrp = HG_HEADS // nheads
    w = nheads * GROUP
    col = lambda k: (lambda b, h: (b, k * ngrp + h))
    big = lambda k: pl.BlockSpec((seq, w), col(k), pipeline_mode=pl.Buffered(1))
    st_spec = pl.BlockSpec((1, nheads, GROUP, GROUP), lambda b, h: (b, h, 0, 0))
    return pl.pallas_call(
        functools.partial(_hgrn_kernel, seq=seq),
        grid=(batch, ngrp),
        in_specs=[big(1), big(2), big(3), big(4), big(5),
                  pl.BlockSpec((4, w), lambda b, h: (0, h)),
                  pl.BlockSpec((1, GROUP), lambda b, h: (0, 0)),
                  st_spec, st_spec],
        out_specs=pl.BlockSpec((seq, w), lambda b, h: (b, h)),
        out_shape=jax.ShapeDtypeStruct((batch * seq, HG_HEADS * GROUP), BF16),
        scratch_shapes=[pltpu.VMEM((2 * nheads, GROUP, GROUP), F32),
                        pltpu.VMEM((seq, w), F32), pltpu.VMEM((seq, w), F32)],
        name="hgrn",
        compiler_params=_params(("arbitrary", "arbitrary")),
    )(z, z, z, z, z, lbl, norm_g, sf, sb)


def _merge_kernel(x_ref, yp_ref, yh_ref, gp_ref, gh_ref, g1_ref, sh_ref, sc_ref, ng_ref,
                  wa_ref, wb_ref, wo_ref, wqt_ref, keys_ref, x1_ref, h2_ref, st_ref):
    m = (_sigmoid(gp_ref[...]) * _dot(yp_ref[...], wa_ref[...])
         + _sigmoid(gh_ref[...]) * _dot(yh_ref[...], wb_ref[...]))
    x1 = x_ref[...] + g1_ref[0] * _dot(m.astype(BF16), wo_ref[...])
    x1_ref[...] = x1
    h2 = _rms(x1) * ng_ref[...] * (1.0 + sc_ref[0]) + sh_ref[0]
    h2_ref[...] = h2.astype(BF16)
    qt = _dot(wqt_ref[...], h2.T.astype(BF16))
    for hp in range(2 * PEER_HEADS):
        st_ref[hp] = _dot(keys_ref[hp], qt[hp * GROUP:(hp + 1) * GROUP].astype(BF16))


def _merge_call(x2d, ypool, yhg, z, mod3, norm2_g, wa16, wb16, wo16, wqt16, keys16, seq, tm):
    r, d = x2d.shape
    gcol0 = (z.shape[1] - 2 * d) // d
    modrow = lambda k: (lambda i: (k + N_MOD * ((i * tm) // seq), 0, 0))
    const2 = lambda i: (0, 0)
    nhp = 2 * PEER_HEADS
    return pl.pallas_call(
        _merge_kernel,
        grid=(r // tm,),
        in_specs=[pl.BlockSpec((tm, d), lambda i: (i, 0)),
                  pl.BlockSpec((tm, ypool.shape[1]), lambda i: (i, 0)),
                  pl.BlockSpec((tm, yhg.shape[1]), lambda i: (i, 0)),
                  pl.BlockSpec((tm, d), lambda i: (i, gcol0)),
                  pl.BlockSpec((tm, d), lambda i: (i, gcol0 + 1)),
                  pl.BlockSpec((1, 1, d), modrow(2)),
                  pl.BlockSpec((1, 1, d), modrow(3)),
                  pl.BlockSpec((1, 1, d), modrow(4)),
                  pl.BlockSpec((1, d), const2),
                  pl.BlockSpec(wa16.shape, const2),
                  pl.BlockSpec(wb16.shape, const2),
                  pl.BlockSpec(wo16.shape, const2),
                  pl.BlockSpec(wqt16.shape, const2),
                  pl.BlockSpec(keys16.shape, lambda i: (0, 0, 0))],
        out_specs=[pl.BlockSpec((tm, d), lambda i: (i, 0)),
                   pl.BlockSpec((tm, d), lambda i: (i, 0)),
                   pl.BlockSpec((nhp, PEER_NKEYS, tm), lambda i: (0, 0, i))],
        out_shape=[jax.ShapeDtypeStruct((r, d), F32),
                   jax.ShapeDtypeStruct((r, d), BF16),
                   jax.ShapeDtypeStruct((nhp, PEER_NKEYS, r), F32)],
        name="merge",
        compiler_params=_params(("arbitrary",)),
    )(x2d, ypool, yhg, z, z, mod3, mod3, mod3, norm2_g, wa16, wb16, wo16, wqt16, keys16)


def _top16(v, iota_n):
    vals, idxs = [], []
    for _ in range(PEER_TOPK):
        m = jnp.max(v, axis=0, keepdims=True)
        idx = jnp.min(jnp.where(v == m, iota_n, float(PEER_NKEYS)), axis=0, keepdims=True)
        v = jnp.where(iota_n == idx, -jnp.inf, v)
        vals.append(m)
        idxs.append(idx)
    return jnp.concatenate(vals, axis=0), jnp.concatenate(idxs, axis=0)


def _topk_unit(s1, s2):
    tg = s1.shape[1]
    kk = PEER_TOPK
    iota_n = lax.broadcasted_iota(I32, (PEER_NKEYS, tg), 0).astype(F32)
    nrow = kk + (kk - 1) * 8
    r = lax.broadcasted_iota(I32, (nrow, tg), 0)
    a_i = jnp.where(r < kk, 0, 1 + ((r - kk) >> 3))
    b_i = jnp.where(r < kk, r, (r - kk) & 7)
    valid = (a_i + 1) * (b_i + 1) <= kk
    pos = (a_i * kk + b_i).astype(F32)

    v1, n1 = _top16(s1, iota_n)
    v2, n2 = _top16(s2, iota_n)
    cand = [v1[0:1] + v2]
    eid = [n1[0:1] * float(PEER_NKEYS) + n2]
    for a in range(1, kk):
        cand.append(v1[a:a + 1] + v2[0:8])
        eid.append(n1[a:a + 1] * float(PEER_NKEYS) + n2[0:8])
    cand = jnp.where(valid, jnp.concatenate(cand, axis=0), -jnp.inf)
    eid = jnp.concatenate(eid, axis=0)
    tops, es = [], []
    for _ in range(kk):
        m = jnp.max(cand, axis=0, keepdims=True)
        p = jnp.min(jnp.where(cand == m, pos, 1e9), axis=0, keepdims=True)
        sel = pos == p
        es.append(jnp.max(jnp.where(sel, eid, -1.0), axis=0, keepdims=True))
        cand = jnp.where(sel, -jnp.inf, cand)
        tops.append(m)
    top = jnp.concatenate(tops, axis=0)
    e = jnp.concatenate(es, axis=0)
    ex = jnp.exp(top - top[0:1])
    gate = ex / jnp.sum(ex, axis=0, keepdims=True)
    e1 = jnp.floor(e * (1.0 / PEER_NKEYS))
    return e1, e - e1 * float(PEER_NKEYS), gate


def _peer_kernel(s_ref, h2_ref, ut_ref, v_ref, x1_ref, g2_ref, fg_ref, o_ref,
                 w_scr, stage_scr, acc_scr, rt_scr, rr_scr):
    tb, d = h2_ref.shape
    eb = v_ref.shape[0]
    i = pl.program_id(0)
    e = pl.program_id(1)
    nk = PEER_NKEYS
    ngrp = tb // GROUP
    pack = w_scr.shape[2]
    unroll = stage_scr.shape[0]
    units = s_ref.shape[0] // 2
    steps_per_grp = PEER_HEADS // units

    @pl.when((i == 0) & (e == 0))
    def _init():
        rt_scr[...] = jnp.zeros(rt_scr.shape, F32)

    @pl.when(e == 0)
    def _build():
        rd = lax.rem(i + 1, 2)
        for fld in range(3):
            for g in range(ngrp):
                rr_scr[fld, g * GROUP:(g + 1) * GROUP, :] = rt_scr[rd, fld, g].T
        iota_s = lax.broadcasted_iota(I32, (nk, nk), 0).astype(F32).astype(BF16)
        zero = jnp.zeros((), BF16)
        one = jnp.ones((), BF16)

        def flush(tt):
            for g in range(unroll // pack):
                blk = stage_scr[g * pack:(g + 1) * pack].astype(BF16)
                w_scr[tt * (unroll // pack) + g] = jnp.transpose(blk, (1, 0, 2))

        def body(tt, carry):
            flush(jnp.maximum(tt - 1, 0))
            ts = [tt * unroll + r for r in range(unroll)]
            row = lambda fld, t: jnp.broadcast_to(rr_scr[fld, pl.ds(t, 1), :], (pack, nk)).astype(BF16)
            tiles = [iota_s[v * pack:(v + 1) * pack] for v in range(nk // pack)]
            onehot = lambda idx, val: jnp.concatenate([jnp.where(tl == idx, val, zero) for tl in tiles], axis=0)
            g1t = [onehot(row(0, t), row(2, t)) for t in ts]
            p2t = [onehot(row(1, t), one) for t in ts]
            for r, (g, p) in enumerate(zip(g1t, p2t)):
                stage_scr[r] = _dot_nt(g, p)
            return carry

        stage_scr[...] = jnp.zeros(stage_scr.shape, F32)
        ntrip = tb // unroll
        lax.fori_loop(0, ntrip, body, 0)
        flush(ntrip - 1)
        acc_scr[...] = jnp.zeros(acc_scr.shape, F32)

    grp = e // steps_per_grp
    head0 = lax.rem(e, steps_per_grp) * units
    wr = lax.rem(i, 2)
    for k in range(units):
        i1, i2, gate = _topk_unit(s_ref[2 * k], s_ref[2 * k + 1])
        rows = pl.ds(pl.multiple_of((head0 + k) * PEER_TOPK, PEER_TOPK), PEER_TOPK)
        rt_scr[wr, 0, grp, rows, :] = i1
        rt_scr[wr, 1, grp, rows, :] = i2
        rt_scr[wr, 2, grp, rows, :] = gate

    n1_0 = e * (eb // nk)
    pieces = []
    for k2 in range(eb // (2 * nk)):
        a2 = _dot(h2_ref[...], ut_ref[:, 2 * k2 * nk:2 * (k2 + 1) * nk])
        for k in (2 * k2, 2 * k2 + 1):
            ak = a2[:, (k - 2 * k2) * nk:(k - 2 * k2 + 1) * nk]
            wn = w_scr[:, n1_0 + k, :, :].reshape(tb, nk)
            act = ak * (0.5 * lax.erf(ak * _SQRT_HALF) + 0.5)
            pieces.append(wn * act.astype(BF16))
    wg = jnp.concatenate(pieces, axis=1)
    tn = 256
    for j in range(d // tn):
        acc_scr[:, j * tn:(j + 1) * tn] += _dot(wg, v_ref[:, j * tn:(j + 1) * tn])

    @pl.when(e == pl.num_programs(1) - 1)
    def _fin():
        x2 = x1_ref[...] + g2_ref[0] * acc_scr[...]
        o_ref[...] = _rms(x2) * fg_ref[...]


def _peer_call(st, h2, ut16, v16, x1, mod3, final_g, seq, tb, eb):
    r, d = h2.shape
    ne = v16.shape[0]
    nblk = r // tb
    ngrp = tb // GROUP
    nsteps = ne // eb
    units = PEER_HEADS * ngrp // nsteps
    assert units * nsteps == PEER_HEADS * ngrp and PEER_HEADS % units == 0
    steps_per_grp = PEER_HEADS // units
    slots = PEER_HEADS * PEER_TOPK
    pack = 16
    prev = lambda i: jnp.maximum(i - 1, 0)
    tok = lambda i, e: (prev(i), 0)
    return pl.pallas_call(
        _peer_kernel,
        grid=(nblk + 1, nsteps),
        in_specs=[pl.BlockSpec((2 * units, PEER_NKEYS, GROUP),
                               lambda i, e: (lax.rem(e, steps_per_grp), 0,
                                             jnp.minimum(i, nblk - 1) * ngrp + e // steps_per_grp)),
                  pl.BlockSpec((tb, d), tok),
                  pl.BlockSpec((d, eb), lambda i, e: (0, e)),
                  pl.BlockSpec((eb, d), lambda i, e: (e, 0)),
                  pl.BlockSpec((tb, d), tok, pipeline_mode=pl.Buffered(1)),
                  pl.BlockSpec((1, 1, d), lambda i, e: (5 + N_MOD * ((prev(i) * tb) // seq), 0, 0)),
                  pl.BlockSpec((1, d), lambda i, e: (0, 0))],
        out_specs=pl.BlockSpec((tb, d), tok),
        out_shape=jax.ShapeDtypeStruct((r, d), F32),
        scratch_shapes=[pltpu.VMEM((tb // pack, PEER_NKEYS, pack, PEER_NKEYS), BF16),
                        pltpu.VMEM((2 * pack, PEER_NKEYS, PEER_NKEYS), F32),
                        pltpu.VMEM((tb, d), F32),
                        pltpu.VMEM((2, 3, ngrp, slots, GROUP), F32),
                        pltpu.VMEM((3, tb, slots), F32)],
        name="peer",
        compiler_params=_params(("arbitrary", "arbitrary")),
    )(st, h2, ut16, v16, x1, mod3, final_g)


def kernel(x, c, ctx, c_ctx, w_mod, b_mod, norm1_g, norm2_g, w_in, pool_w, pool_scale, hg_lb_logits, hg_norm_g,
           w_a, w_b, w_o, peer_wq, peer_keys, peer_u, peer_v, final_g):
    batch, seq, d = x.shape
    ctx_len = ctx.shape[1]
    assert w_mod.shape[0] == 1 and seq == GRID_W * GRID_W and hg_lb_logits.shape[1] == 2
    hw = HG_HEADS * GROUP

    cvec = jnp.concatenate([c, c_ctx[None, :], jnp.zeros((8 - batch - 1, d), F32)], axis=0)
    mod = _mod_call(cvec, w_mod[0], b_mod[0][None, :])
    mod3 = mod.reshape(8 * N_MOD, 1, d)

    w_in16 = w_in[0].astype(BF16)
    g1n = norm1_g[0][None, :]
    x2d = x.reshape(batch * seq, d)
    z = _inproj_call(x2d, mod3, 0, 1, seq, g1n, w_in16, 0, w_in16.shape[1] // 1024, 1024, 1024)
    zc = _inproj_call(ctx.reshape(batch * ctx_len, d), mod3, N_MOD * batch, N_MOD * batch + 1, None, g1n,
                      w_in16, 1, 2, 256, 1024)

    lbl = hg_lb_logits.reshape(2 * hg_lb_logits.shape[1], hw)
    sf, sb = _hgrn_ctx_call(zc, lbl, batch, ctx_len)
    yhg = _hgrn_call(z, lbl, hg_norm_g[0][None, :], sf, sb, batch, seq, 2)
    ypool = _pool_call(z, pool_w[0], pool_scale[0][None, :], batch, seq)

    nhp = 2 * PEER_HEADS
    x1, h2, st = _merge_call(x2d, ypool, yhg, z, mod3, norm2_g[0][None, :],
                             w_a[0].astype(BF16), w_b[0].astype(BF16), w_o[0].astype(BF16),
                             peer_wq[0].T.astype(BF16),
                             peer_keys[0].reshape(nhp, PEER_NKEYS, GROUP).astype(BF16), seq, 256)
    out = _peer_call(st, h2, peer_u[0].T.astype(BF16), peer_v[0].astype(BF16), x1, mod3,
                     final_g[None, :], seq, 512, 2048)
    return out.reshape(batch, seq, d)
```

```python
import functools
import math

import jax
import jax.numpy as jnp
from jax import lax
from jax.experimental import pallas as pl
from jax.experimental.pallas import tpu as pltpu

F32 = jnp.float32
BF16 = jnp.bfloat16
I32 = jnp.int32

EPS = 1e-6
GRID_W = 64
POOL_WINDOWS = (2, 4, 8, 16)
POOL_PAD = 8
GROUP = 128
HG_HEADS = 4
HG_CHUNK = 16
HG_BLOCK = 128
HG_MID = 8
PEER_HEADS = 8
PEER_NKEYS = 128
PEER_TOPK = 16
N_MOD = 6
VMEM_LIMIT = 60 * 1024 * 1024

_NT = (((1,), (1,)), ((), ()))
_SQRT_HALF = math.sqrt(0.5)


def _dot(a, b):
    return jnp.dot(a, b, preferred_element_type=F32)


def _dot_nt(a, b):
    return lax.dot_general(a, b, _NT, preferred_element_type=F32)


def _hi_lo(a):
    hi = a.astype(BF16)
    return hi, (a - hi.astype(F32)).astype(BF16)


def _sigmoid(x):
    return 1.0 / (1.0 + jnp.exp(-x))


def _rms(x):
    return x * lax.rsqrt(jnp.mean(x * x, axis=-1, keepdims=True) + EPS)


def _params(sem):
    return pltpu.CompilerParams(dimension_semantics=sem, vmem_limit_bytes=VMEM_LIMIT)


def _mod_kernel(c_ref, w_ref, b_ref, o_ref):
    c = c_ref[...]
    s = c * _sigmoid(c)
    s_hi, s_lo = _hi_lo(s)
    w_hi, w_lo = _hi_lo(w_ref[...])
    o_ref[...] = _dot(s_hi, w_hi) + (_dot(s_hi, w_lo) + _dot(s_lo, w_hi)) + b_ref[...]


def _mod_call(cvec, w_mod, b_mod):
    rows, d = cvec.shape
    n = w_mod.shape[1]
    tn = n // 4
    return pl.pallas_call(
        _mod_kernel,
        grid=(n // tn,),
        in_specs=[pl.BlockSpec((rows, d), lambda j: (0, 0)),
                  pl.BlockSpec((d, tn), lambda j: (0, j)),
                  pl.BlockSpec((1, tn), lambda j: (0, j))],
        out_specs=pl.BlockSpec((rows, tn), lambda j: (0, j)),
        out_shape=jax.ShapeDtypeStruct((rows, n), F32),
        name="mod",
        compiler_params=_params(("arbitrary",)),
    )(cvec, w_mod, b_mod)


def _inproj_kernel(x_ref, sh_ref, sc_ref, g_ref, w_ref, o_ref, h_scr):
    @pl.when(pl.program_id(1) == 0)
    def _():
        h = _rms(x_ref[...]) * g_ref[...] * (1.0 + sc_ref[0]) + sh_ref[0]
        h_scr[...] = h.astype(BF16)

    o_ref[...] = _dot(h_scr[...], w_ref[...])


def _inproj_call(x2d, mod3, sh_row, sc_row, rows_per_mod, g, w16, col0, ncols, tm, tn):
    r, d = x2d.shape
    if rows_per_mod is None:
        sh_map = lambda i, j: (sh_row, 0, 0)
        sc_map = lambda i, j: (sc_row, 0, 0)
    else:
        sh_map = lambda i, j: (sh_row + N_MOD * ((i * tm) // rows_per_mod), 0, 0)
        sc_map = lambda i, j: (sc_row + N_MOD * ((i * tm) // rows_per_mod), 0, 0)
    return pl.pallas_call(
        _inproj_kernel,
        grid=(r // tm, ncols),
        in_specs=[pl.BlockSpec((tm, d), lambda i, j: (i, 0)),
                  pl.BlockSpec((1, 1, d), sh_map),
                  pl.BlockSpec((1, 1, d), sc_map),
                  pl.BlockSpec((1, d), lambda i, j: (0, 0)),
                  pl.BlockSpec((d, tn), lambda i, j: (0, j + col0))],
        out_specs=pl.BlockSpec((tm, tn), lambda i, j: (i, j)),
        out_shape=jax.ShapeDtypeStruct((r, ncols * tn), F32),
        scratch_shapes=[pltpu.VMEM((tm, d), BF16)],
        name="inproj",
        compiler_params=_params(("arbitrary", "arbitrary")),
    )(x2d, mod3, mod3, g, w16)


def _pool_kernel(p_ref, w_ref, sc_ref, o_ref, pad_scr, tmp_scr):
    gw, pp = GRID_W, POOL_PAD
    pad_scr[...] = jnp.zeros(pad_scr.shape, F32)
    r_i = lax.broadcasted_iota(I32, (gw, gw, GROUP), 0)
    c_i = lax.broadcasted_iota(I32, (gw, gw, GROUP), 1)
    for g, win in enumerate(POOL_WINDOWS):
        half = win // 2
        pg = p_ref[:, g * GROUP:(g + 1) * GROUP].reshape(gw, gw, GROUP)
        pad_scr[pp:pp + gw, pp:pp + gw, :] = pg
        acc = None
        for k in range(-half, win - half):
            sl = pad_scr[pp + k:pp + k + gw, :, :]
            acc = sl if acc is None else acc + sl
        tmp_scr[...] = acc
        acc = None
        for k in range(-half, win - half):
            sl = tmp_scr[:, pp + k:pp + k + gw, :]
            acc = sl if acc is None else acc + sl
        cr = jnp.minimum(r_i - half + win, gw) - jnp.maximum(r_i - half, 0)
        cc = jnp.minimum(c_i - half + win, gw) - jnp.maximum(c_i - half, 0)
        mean = acc / (cr * cc).astype(F32)
        d = (mean - pg).reshape(gw * gw, GROUP)
        y = _dot(d.astype(BF16), w_ref[g].astype(BF16)) * sc_ref[:, g * GROUP:(g + 1) * GROUP]
        o_ref[:, g * GROUP:(g + 1) * GROUP] = y.astype(BF16)


def _pool_call(z, pool_w, pool_scale, batch, seq):
    width = len(POOL_WINDOWS) * GROUP
    ext = GRID_W + 2 * POOL_PAD
    return pl.pallas_call(
        _pool_kernel,
        grid=(batch,),
        in_specs=[pl.BlockSpec((seq, width), lambda b: (b, 0)),
                  pl.BlockSpec(pool_w.shape, lambda b: (0, 0, 0)),
                  pl.BlockSpec((1, width), lambda b: (0, 0))],
        out_specs=pl.BlockSpec((seq, width), lambda b: (b, 0)),
        out_shape=jax.ShapeDtypeStruct((batch * seq, width), BF16),
        scratch_shapes=[pltpu.VMEM((ext, ext, GROUP), F32), pltpu.VMEM((GRID_W, ext, GROUP), F32)],
        name="pool",
        compiler_params=_params(("arbitrary",)),
    )(z, pool_w, pool_scale)


def _hgrn_masks(rev):
    t, c = HG_BLOCK, HG_CHUNK
    row = lax.broadcasted_iota(I32, (t, t), 0)
    col = lax.broadcasted_iota(I32, (t, t), 1)
    sh = c.bit_length() - 1
    same = (row >> sh) == (col >> sh)
    tri = (col >= row) if rev else (col <= row)
    mid = ((row >> sh) << sh) + HG_MID
    to_mid = (col >= mid) if rev else (col <= mid)
    m_all = jnp.concatenate([jnp.where(same & tri, 1.0, 0.0),
                             jnp.where(same, 1.0, 0.0),
                             jnp.where(same & to_mid, 1.0, 0.0)], axis=0).astype(BF16)
    return m_all, same & tri, col >> sh


def _hgrn_blocks(chains, masks, emit_out):
    t, c = HG_BLOCK, HG_CHUNK
    nch = t // c
    nc = len(chains)
    revs = [ch[0] for ch in chains]
    qs = [ch[2] for ch in chains]
    vs = [ch[3] for ch in chains]
    sts = [ch[5] for ch in chains]
    ks, bs = [], []
    for rev, x_f, _, _, lb, _ in chains:
        f = lb + (1.0 - lb) * _sigmoid(x_f)
        g_hi, g_lo = _hi_lo(jnp.log(f))
        m_all = masks[rev][0]
        ks.append(1.0 - f)
        bs.append(_dot(m_all, g_hi) + _dot(m_all, g_lo))
    b_loc = [b[0:t] for b in bs]
    b_tot = [b[t:2 * t] for b in bs]
    b_mid = [b[2 * t:3 * t] for b in bs]
    kd16 = [(ks[i] * jnp.exp(b_tot[i] - b_loc[i])).astype(BF16) for i in range(nc)]
    vt = [v.T for v in vs]
    a_all = [jnp.exp(b) for b in b_tot]
    os_ = [None] * nc
    if emit_out:
        a16 = [(qs[i] * jnp.exp(b_loc[i])).astype(BF16) for i in range(nc)]
        qd16 = [(qs[i] * jnp.exp(b_loc[i] - b_mid[i])).astype(BF16) for i in range(nc)]
        kg16 = [(ks[i] * jnp.exp(b_mid[i] - b_loc[i])).astype(BF16) for i in range(nc)]
        attn = [jnp.where(masks[revs[i]][1], _dot_nt(qd16[i], kg16[i]), 0.0) for i in range(nc)]
        os_ = [_dot(attn[i].astype(BF16), vs[i].astype(BF16)) for i in range(nc)]
        o_inter = [[None] * nch for _ in range(nc)]
    for step in range(nch):
        for i in range(nc):
            n = nch - 1 - step if revs[i] else step
            if emit_out:
                o_inter[i][n] = _dot_nt(a16[i][n * c:(n + 1) * c], sts[i].astype(BF16))
            ut = _dot(jnp.where(masks[revs[i]][2] == n, vt[i], 0.0).astype(BF16), kd16[i])
            sts[i] = sts[i] * a_all[i][n * c:n * c + 1, :] + ut
    if emit_out:
        os_ = [os_[i] + jnp.concatenate(o_inter[i], axis=0) for i in range(nc)]
    return list(zip(os_, sts))


def _lower_bounds(lbl_ref):
    out = []
    for d in range(2):
        l0, l1 = lbl_ref[2 * d:2 * d + 1, :], lbl_ref[2 * d + 1:2 * d + 2, :]
        m = jnp.maximum(l0, l1)
        e0, e1 = jnp.exp(l0 - m), jnp.exp(l1 - m)
        out.append(e0 / (e0 + e1))
    return out


def _hgrn_run(nheads, seq, lbl_ref, f_refs, q_ref, v_ref, st_scr, o_scrs):
    t = HG_BLOCK
    nblk = seq // t
    masks = (_hgrn_masks(False), _hgrn_masks(True))
    lbs = _lower_bounds(lbl_ref)
    emit_out = o_scrs is not None

    def body(i, carry):
        chains = []
        for d in range(2):
            blk = (nblk - 1 - i) if d else i
            rows = pl.ds(pl.multiple_of(blk * t, t), t)
            for h in range(nheads):
                ls = slice(h * GROUP, (h + 1) * GROUP)
                q = q_ref[rows, ls] if emit_out else None
                chains.append((d, h, rows, ls, f_refs[d][rows, ls], q, v_ref[rows, ls], st_scr[d * nheads + h]))
        results = _hgrn_blocks([(d, x_f, q, v, lbs[d][:, ls], st) for d, h, rows, ls, x_f, q, v, st in chains],
                               masks, emit_out)
        for (d, h, rows, ls, *_), (o, st) in zip(chains, results):
            st_scr[d * nheads + h] = st
            if emit_out:
                o_scrs[d][rows, ls] = o
        return carry

    lax.fori_loop(0, nblk, body, 0)


def _hgrn_ctx_kernel(v_ref, ff_ref, fb_ref, lbl_ref, sf_ref, sb_ref, st_scr, *, seq):
    nheads = sf_ref.shape[1]
    st_scr[...] = jnp.zeros(st_scr.shape, F32)
    _hgrn_run(nheads, seq, lbl_ref, (ff_ref, fb_ref), None, v_ref, st_scr, None)
    sf_ref[0] = st_scr[0:nheads]
    sb_ref[0] = st_scr[nheads:2 * nheads]


def _hgrn_kernel(q_ref, v_ref, ff_ref, fb_ref, og_ref, lbl_ref, ng_ref, sf_ref, sb_ref, y_ref,
                 st_scr, of_scr, ob_scr, *, seq):
    nheads = sf_ref.shape[1]
    st_scr[0:nheads] = sf_ref[0]
    st_scr[nheads:2 * nheads] = sb_ref[0]
    _hgrn_run(nheads, seq, lbl_ref, (ff_ref, fb_ref), q_ref, v_ref, st_scr, (of_scr, ob_scr))
    rb = 4 * HG_BLOCK

    def epilogue(i, carry):
        rows = pl.ds(pl.multiple_of(i * rb, rb), rb)
        for h in range(nheads):
            ls = slice(h * GROUP, (h + 1) * GROUP)
            og = og_ref[rows, ls]
            y = _rms(of_scr[rows, ls] + ob_scr[rows, ls]) * ng_ref[...] * (og * _sigmoid(og))
            y_ref[rows, ls] = y.astype(BF16)
        return carry

    lax.fori_loop(0, seq // rb, epilogue, 0)


def _hgrn_ctx_call(zc, lbl, batch, seq):
    hw = HG_HEADS * GROUP
    st_spec = pl.BlockSpec((1, HG_HEADS, GROUP, GROUP), lambda b: (b, 0, 0, 0))
    st_shape = jax.ShapeDtypeStruct((batch, HG_HEADS, GROUP, GROUP), F32)
    return pl.pallas_call(
        functools.partial(_hgrn_ctx_kernel, seq=seq),
        grid=(batch,),
        in_specs=[pl.BlockSpec((seq, hw), lambda b: (b, 0)),
                  pl.BlockSpec((seq, hw), lambda b: (b, 1)),
                  pl.BlockSpec((seq, hw), lambda b: (b, 2)),
                  pl.BlockSpec((4, hw), lambda b: (0, 0))],
        out_specs=[st_spec, st_spec],
        out_shape=[st_shape, st_shape],
        scratch_shapes=[pltpu.VMEM((2 * HG_HEADS, GROUP, GROUP), F32)],
        name="hgrn_ctx",
        compiler_params=_params(("arbitrary",)),
    )(zc, zc, zc, lbl)


def _hgrn_call(z, lbl, norm_g, sf, sb, batch, seq, nheads):
    ngrp = HG_HEADS // nheads
    w = nheads * GROUP
    col = lambda k: (lambda b, h: (b, k * ngrp + h))
    big = lambda k: pl.BlockSpec((seq, w), col(k), pipeline_mode=pl.Buffered(1))
    st_spec = pl.BlockSpec((1, nheads, GROUP, GROUP), lambda b, h: (b, h, 0, 0))
    return pl.pallas_call(
        functools.partial(_hgrn_kernel, seq=seq),
        grid=(batch, ngrp),
        in_specs=[big(1), big(2), big(3), big(4), big(5),
                  pl.BlockSpec((4, w), lambda b, h: (0, h)),
                  pl.BlockSpec((1, GROUP), lambda b, h: (0, 0)),
                  st_spec, st_spec],
        out_specs=pl.BlockSpec((seq, w), lambda b, h: (b, h)),
        out_shape=jax.ShapeDtypeStruct((batch * seq, HG_HEADS * GROUP), BF16),
        scratch_shapes=[pltpu.VMEM((2 * nheads, GROUP, GROUP), F32),
                        pltpu.VMEM((seq, w), F32), pltpu.VMEM((seq, w), F32)],
        name="hgrn",
        compiler_params=_params(("arbitrary", "arbitrary")),
    )(z, z, z, z, z, lbl, norm_g, sf, sb)


def _merge_kernel(x_ref, yp_ref, yh_ref, gp_ref, gh_ref, g1_ref, sh_ref, sc_ref, ng_ref,
                  wa_ref, wb_ref, wo_ref, wqt_ref, keys_ref, x1_ref, h2_ref, st_ref):
    m = (_sigmoid(gp_ref[...]) * _dot(yp_ref[...], wa_ref[...])
         + _sigmoid(gh_ref[...]) * _dot(yh_ref[...], wb_ref[...]))
    x1 = x_ref[...] + g1_ref[0] * _dot(m.astype(BF16), wo_ref[...])
    x1_ref[...] = x1
    h2 = _rms(x1) * ng_ref[...] * (1.0 + sc_ref[0]) + sh_ref[0]
    h2_ref[...] = (h2 * _SQRT_HALF).astype(BF16)
    qt = _dot(wqt_ref[...], h2.T.astype(BF16))
    for hp in range(2 * PEER_HEADS):
        st_ref[hp] = _dot(keys_ref[hp], qt[hp * GROUP:(hp + 1) * GROUP].astype(BF16))


def _merge_call(x2d, ypool, yhg, z, mod3, norm2_g, wa16, wb16, wo16, wqt16, keys16, seq, tm):
    r, d = x2d.shape
    gcol0 = (z.shape[1] - 2 * d) // d
    modrow = lambda k: (lambda i: (k + N_MOD * ((i * tm) // seq), 0, 0))
    const2 = lambda i: (0, 0)
    nhp = 2 * PEER_HEADS
    return pl.pallas_call(
        _merge_kernel,
        grid=(r // tm,),
        in_specs=[pl.BlockSpec((tm, d), lambda i: (i, 0)),
                  pl.BlockSpec((tm, ypool.shape[1]), lambda i: (i, 0)),
                  pl.BlockSpec((tm, yhg.shape[1]), lambda i: (i, 0)),
                  pl.BlockSpec((tm, d), lambda i: (i, gcol0)),
                  pl.BlockSpec((tm, d), lambda i: (i, gcol0 + 1)),
                  pl.BlockSpec((1, 1, d), modrow(2)),
                  pl.BlockSpec((1, 1, d), modrow(3)),
                  pl.BlockSpec((1, 1, d), modrow(4)),
                  pl.BlockSpec((1, d), const2),
                  pl.BlockSpec(wa16.shape, const2),
                  pl.BlockSpec(wb16.shape, const2),
                  pl.BlockSpec(wo16.shape, const2),
                  pl.BlockSpec(wqt16.shape, const2),
                  pl.BlockSpec(keys16.shape, lambda i: (0, 0, 0))],
        out_specs=[pl.BlockSpec((tm, d), lambda i: (i, 0)),
                   pl.BlockSpec((tm, d), lambda i: (i, 0)),
                   pl.BlockSpec((nhp, PEER_NKEYS, tm), lambda i: (0, 0, i))],
        out_shape=[jax.ShapeDtypeStruct((r, d), F32),
                   jax.ShapeDtypeStruct((r, d), BF16),
                   jax.ShapeDtypeStruct((nhp, PEER_NKEYS, r), F32)],
        name="merge",
        compiler_params=_params(("arbitrary",)),
    )(x2d, ypool, yhg, z, z, mod3, mod3, mod3, norm2_g, wa16, wb16, wo16, wqt16, keys16)


def _top16(v, iota_n):
    vals, idxs = [], []
    for _ in range(PEER_TOPK):
        m = jnp.max(v, axis=0, keepdims=True)
        idx = jnp.min(jnp.where(v == m, iota_n, float(PEER_NKEYS)), axis=0, keepdims=True)
        v = jnp.where(iota_n == idx, -jnp.inf, v)
        vals.append(m)
        idxs.append(idx)
    return jnp.concatenate(vals, axis=0), jnp.concatenate(idxs, axis=0)


def _topk_unit(s1, s2):
    tg = s1.shape[1]
    kk = PEER_TOPK
    nkf = float(PEER_NKEYS)
    iota_n = lax.broadcasted_iota(I32, (PEER_NKEYS, tg), 0).astype(F32)
    v1, n1 = _top16(s1, iota_n)
    v2, n2 = _top16(s2, iota_n)

    r = lax.broadcasted_iota(I32, (8, tg), 0)
    rf = r.astype(F32)
    sel3 = lambda x: jnp.where(r < 5, x[0], x[1])
    sel4 = lambda x: jnp.where(r < 4, x[0], jnp.where(r < 6, x[1], x[2]))
    sel5 = lambda x: jnp.where(r < 2, x[0], x[1])

    def tiles(l1, l2):
        row1 = lambda a: l1[a:a + 1]
        lo2, hi1 = l2[0:8], l1[8:16]
        hi1_r2 = pltpu.roll(hi1, 2, 0)
        t1 = [row1(0), row1(0), row1(1), sel3((row1(2), row1(4))), sel4((row1(3), row1(5), row1(6))),
              sel5((row1(7), hi1_r2)), hi1_r2]
        t2 = [lo2, l2[8:16], lo2, sel3((lo2, pltpu.roll(lo2, 5, 0))),
              sel4((lo2, pltpu.roll(lo2, 4, 0), pltpu.roll(lo2, 6, 0))), sel5((lo2, l2[0:1])), l2[0:1]]
        return t1, t2

    tv1, tv2 = tiles(v1, v2)
    tn1, tn2 = tiles(n1, n2)
    cand = [a + b for a, b in zip(tv1, tv2)]
    cand[6] = jnp.where(r < 2, cand[6], -jnp.inf)
    cand = jnp.concatenate(cand, axis=0)
    eid = jnp.concatenate([a * nkf + b for a, b in zip(tn1, tn2)], axis=0)
    pos = jnp.concatenate([rf, rf + 8.0, rf + 16.0, sel3((rf + 32.0, rf + 59.0)),
                           sel4((rf + 48.0, rf + 76.0, rf + 90.0)), sel5((rf + 112.0, rf * 16.0 + 96.0)),
                           rf * 16.0 + 224.0], axis=0)
    tops, es = [], []
    for _ in range(kk):
        m = jnp.max(cand, axis=0, keepdims=True)
        p = jnp.min(jnp.where(cand == m, pos, 1e9), axis=0, keepdims=True)
        sel = pos == p
        es.append(jnp.max(jnp.where(sel, eid, -1.0), axis=0, keepdims=True))
        cand = jnp.where(sel, -jnp.inf, cand)
        tops.append(m)
    top = jnp.concatenate(tops, axis=0)
    e = jnp.concatenate(es, axis=0)
    ex = jnp.exp(top - top[0:1])
    gate = ex / jnp.sum(ex, axis=0, keepdims=True)
    e1 = jnp.floor(e * (1.0 / PEER_NKEYS))
    return e1, e - e1 * nkf, gate * _SQRT_HALF


def _peer_kernel(s_ref, h2_ref, ut_ref, v_ref, x1_ref, g2_ref, fg_ref, o_ref,
                 w_scr, stage_scr, tr0_scr, tr1_scr, acc_scr, rt_scr, rr_scr):
    tb, d = h2_ref.shape
    eb = v_ref.shape[0]
    i = pl.program_id(0)
    e = pl.program_id(1)
    nk = PEER_NKEYS
    ngrp = tb // GROUP
    pack = w_scr.shape[2]
    unroll = stage_scr.shape[0]
    tr_scrs = (tr0_scr, tr1_scr)
    units = s_ref.shape[0] // 2
    steps_per_grp = PEER_HEADS // units

    @pl.when((i == 0) & (e == 0))
    def _init():
        rt_scr[...] = jnp.zeros(rt_scr.shape, F32)

    @pl.when(e == 0)
    def _build():
        rd = lax.rem(i + 1, 2)
        for fld in range(3):
            for g in range(ngrp):
                rr_scr[fld, g * GROUP:(g + 1) * GROUP, :] = rt_scr[rd, fld, g].T
        iota_s = lax.broadcasted_iota(I32, (nk, nk), 0).astype(F32).astype(BF16)
        zero = jnp.zeros((), BF16)
        one = jnp.ones((), BF16)

        def flush(tt):
            for g in range(unroll // pack):
                blk = stage_scr[g * pack:(g + 1) * pack].astype(BF16)
                tr_scrs[g][...] = jnp.transpose(blk, (1, 0, 2))
            for g in range(unroll // pack):
                w_scr[tt * (unroll // pack) + g] = tr_scrs[g][...]

        def body(tt, carry):
            flush(jnp.maximum(tt - 1, 0))
            ts = [tt * unroll + r for r in range(unroll)]
            row = lambda fld, t: jnp.broadcast_to(rr_scr[fld, pl.ds(t, 1), :], (pack, nk)).astype(BF16)
            tiles = [iota_s[v * pack:(v + 1) * pack] for v in range(nk // pack)]
            onehot = lambda idx, val: jnp.concatenate([jnp.where(tl == idx, val, zero) for tl in tiles], axis=0)
            g1t = [onehot(row(0, t), row(2, t)) for t in ts]
            p2t = [onehot(row(1, t), one) for t in ts]
            for r, (g, p) in enumerate(zip(g1t, p2t)):
                stage_scr[r] = _dot_nt(g, p)
            return carry

        stage_scr[...] = jnp.zeros(stage_scr.shape, F32)
        ntrip = tb // unroll
        lax.fori_loop(0, ntrip, body, 0)
        flush(ntrip - 1)
        acc_scr[...] = jnp.zeros(acc_scr.shape, F32)

    grp = e // steps_per_grp
    head0 = lax.rem(e, steps_per_grp) * units
    wr = lax.rem(i, 2)
    n1_0 = e * (eb // nk)
    nchunk = eb // (2 * nk)
    pieces = []

    def route(k):
        i1, i2, gate = _topk_unit(s_ref[2 * k], s_ref[2 * k + 1])
        rows = pl.ds(pl.multiple_of((head0 + k) * PEER_TOPK, PEER_TOPK), PEER_TOPK)
        rt_scr[wr, 0, grp, rows, :] = i1
        rt_scr[wr, 1, grp, rows, :] = i2
        rt_scr[wr, 2, grp, rows, :] = gate

    def activate(k2):
        a2 = _dot(h2_ref[...], ut_ref[:, 2 * k2 * nk:2 * (k2 + 1) * nk])
        for k in (2 * k2, 2 * k2 + 1):
            ak = a2[:, (k - 2 * k2) * nk:(k - 2 * k2 + 1) * nk]
            wn = w_scr[:, n1_0 + k, :, :].reshape(tb, nk)
            act = ak * (1.0 + lax.erf(ak))
            pieces.append(wn * act.astype(BF16))

    per = nchunk // units
    for k in range(units):
        route(k)
        for k2 in range(k * per, (k + 1) * per):
            activate(k2)
    wg = jnp.concatenate(pieces, axis=1)
    tn = 256
    for j in range(d // tn):
        acc_scr[:, j * tn:(j + 1) * tn] += _dot(wg, v_ref[:, j * tn:(j + 1) * tn])

    @pl.when(e == pl.num_programs(1) - 1)
    def _fin():
        x2 = x1_ref[...] + g2_ref[0] * acc_scr[...]
        o_ref[...] = _rms(x2) * fg_ref[...]


def _peer_call(st, h2, ut16, v16, x1, mod3, final_g, seq, tb, eb):
    r, d = h2.shape
    ne = v16.shape[0]
    nblk = r // tb
    ngrp = tb // GROUP
    nsteps = ne // eb
    units = PEER_HEADS * ngrp // nsteps
    assert units * nsteps == PEER_HEADS * ngrp and PEER_HEADS % units == 0
    steps_per_grp = PEER_HEADS // units
    slots = PEER_HEADS * PEER_TOPK
    pack = 16
    prev = lambda i: jnp.maximum(i - 1, 0)
    tok = lambda i, e: (prev(i), 0)
    return pl.pallas_call(
        _peer_kernel,
        grid=(nblk + 1, nsteps),
        in_specs=[pl.BlockSpec((2 * units, PEER_NKEYS, GROUP),
                               lambda i, e: (lax.rem(e, steps_per_grp), 0,
                                             jnp.minimum(i, nblk - 1) * ngrp + e // steps_per_grp)),
                  pl.BlockSpec((tb, d), tok),
                  pl.BlockSpec((d, eb), lambda i, e: (0, e)),
                  pl.BlockSpec((eb, d), lambda i, e: (e, 0)),
                  pl.BlockSpec((tb, d), tok, pipeline_mode=pl.Buffered(1)),
                  pl.BlockSpec((1, 1, d), lambda i, e: (5 + N_MOD * ((prev(i) * tb) // seq), 0, 0)),
                  pl.BlockSpec((1, d), lambda i, e: (0, 0))],
        out_specs=pl.BlockSpec((tb, d), tok),
        out_shape=jax.ShapeDtypeStruct((r, d), F32),
        scratch_shapes=[pltpu.VMEM((tb // pack, PEER_NKEYS, pack, PEER_NKEYS), BF16),
                        pltpu.VMEM((2 * pack, PEER_NKEYS, PEER_NKEYS), F32),
                        pltpu.VMEM((PEER_NKEYS, pack, PEER_NKEYS), BF16),
                        pltpu.VMEM((PEER_NKEYS, pack, PEER_NKEYS), BF16),
                        pltpu.VMEM((tb, d), F32),
                        pltpu.VMEM((2, 3, ngrp, slots, GROUP), F32),
                        pltpu.VMEM((3, tb, slots), F32)],
        name="peer",
        compiler_params=_params(("arbitrary", "arbitrary")),
    )(st, h2, ut16, v16, x1, mod3, final_g)


def kernel(x, c, ctx, c_ctx, w_mod, b_mod, norm1_g, norm2_g, w_in, pool_w, pool_scale, hg_lb_logits, hg_norm_g,
           w_a, w_b, w_o, peer_wq, peer_keys, peer_u, peer_v, final_g):
    batch, seq, d = x.shape
    ctx_len = ctx.shape[1]
    assert w_mod.shape[0] == 1 and seq == GRID_W * GRID_W and hg_lb_logits.shape[1] == 2
    hw = HG_HEADS * GROUP

    cvec = jnp.concatenate([c, c_ctx[None, :], jnp.zeros((8 - batch - 1, d), F32)], axis=0)
    mod = _mod_call(cvec, w_mod[0], b_mod[0][None, :])
    mod3 = mod.reshape(8 * N_MOD, 1, d)

    w_in16 = w_in[0].astype(BF16)
    g1n = norm1_g[0][None, :]
    x2d = x.reshape(batch * seq, d)
    z = _inproj_call(x2d, mod3, 0, 1, seq, g1n, w_in16, 0, w_in16.shape[1] // 1024, 1024, 1024)
    zc = _inproj_call(ctx.reshape(batch * ctx_len, d), mod3, N_MOD * batch, N_MOD * batch + 1, None, g1n,
                      w_in16, 1, 2, 256, 1024)

    lbl = hg_lb_logits.reshape(2 * hg_lb_logits.shape[1], hw)
    sf, sb = _hgrn_ctx_call(zc, lbl, batch, ctx_len)
    yhg = _hgrn_call(z, lbl, hg_norm_g[0][None, :], sf, sb, batch, seq, 2)
    ypool = _pool_call(z, pool_w[0], pool_scale[0][None, :], batch, seq)

    nhp = 2 * PEER_HEADS
    x1, h2, st = _merge_call(x2d, ypool, yhg, z, mod3, norm2_g[0][None, :],
                             w_a[0].astype(BF16), w_b[0].astype(BF16), w_o[0].astype(BF16),
                             peer_wq[0].T.astype(BF16),
                             peer_keys[0].reshape(nhp, PEER_NKEYS, GROUP).astype(BF16), seq, 256)
    out = _peer_call(st, h2, peer_u[0].T.astype(BF16), peer_v[0].astype(BF16), x1, mod3,
                     final_g[None, :], seq, 512, 2048)
    return out.reshape(batch, seq, d)
```

```python
import functools
import math

import jax
import jax.numpy as jnp
from jax import lax
from jax.experimental import pallas as pl
from jax.experimental.pallas import tpu as pltpu

F32 = jnp.float32
BF16 = jnp.bfloat16
I32 = jnp.int32

EPS = 1e-6
GRID_W = 64
POOL_WINDOWS = (2, 4, 8, 16)
POOL_PAD = 8
GROUP = 128
HG_HEADS = 4
HG_CHUNK = 16
HG_BLOCK = 128
HG_MID = 8
PEER_HEADS = 8
PEER_NKEYS = 128
PEER_TOPK = 16
N_MOD = 6
VMEM_LIMIT = 60 * 1024 * 1024

_NT = (((1,), (1,)), ((), ()))
_SQRT_HALF = math.sqrt(0.5)


def _dot(a, b):
    return jnp.dot(a, b, preferred_element_type=F32)


def _dot_nt(a, b):
    return lax.dot_general(a, b, _NT, preferred_element_type=F32)


def _hi_lo(a):
    hi = a.astype(BF16)
    return hi, (a - hi.astype(F32)).astype(BF16)


def _sigmoid(x):
    return 1.0 / (1.0 + jnp.exp(-x))


def _rms(x):
    return x * lax.rsqrt(jnp.mean(x * x, axis=-1, keepdims=True) + EPS)


def _params(sem):
    return pltpu.CompilerParams(dimension_semantics=sem, vmem_limit_bytes=VMEM_LIMIT)


def _mod_kernel(c_ref, w_ref, b_ref, o_ref):
    c = c_ref[...]
    s = c * _sigmoid(c)
    s_hi, s_lo = _hi_lo(s)
    w_hi, w_lo = _hi_lo(w_ref[...])
    o_ref[...] = _dot(s_hi, w_hi) + (_dot(s_hi, w_lo) + _dot(s_lo, w_hi)) + b_ref[...]


def _mod_call(cvec, w_mod, b_mod):
    rows, d = cvec.shape
    n = w_mod.shape[1]
    tn = n // 4
    return pl.pallas_call(
        _mod_kernel,
        grid=(n // tn,),
        in_specs=[pl.BlockSpec((rows, d), lambda j: (0, 0)),
                  pl.BlockSpec((d, tn), lambda j: (0, j)),
                  pl.BlockSpec((1, tn), lambda j: (0, j))],
        out_specs=pl.BlockSpec((rows, tn), lambda j: (0, j)),
        out_shape=jax.ShapeDtypeStruct((rows, n), F32),
        name="mod",
        compiler_params=_params(("arbitrary",)),
    )(cvec, w_mod, b_mod)


def _inproj_kernel(x_ref, sh_ref, sc_ref, g_ref, w_ref, o16_ref, of_ref, h_scr):
    j = pl.program_id(1)
    last = pl.num_programs(1) - 1

    @pl.when(j == 0)
    def _():
        h = _rms(x_ref[...]) * g_ref[...] * (1.0 + sc_ref[0]) + sh_ref[0]
        h_scr[...] = h.astype(BF16)

    y = _dot(h_scr[...], w_ref[...])

    @pl.when(j < last)
    def _():
        o16_ref[...] = y.astype(BF16)

    @pl.when(j == last)
    def _():
        of_ref[...] = y


def _inproj_call(x2d, mod3, sh_row, sc_row, rows_per_mod, g, w16, tm, tn):
    r, d = x2d.shape
    ncols = w16.shape[1] // tn
    if rows_per_mod is None:
        sh_map = lambda i, j: (sh_row, 0, 0)
        sc_map = lambda i, j: (sc_row, 0, 0)
    else:
        sh_map = lambda i, j: (sh_row + N_MOD * ((i * tm) // rows_per_mod), 0, 0)
        sc_map = lambda i, j: (sc_row + N_MOD * ((i * tm) // rows_per_mod), 0, 0)
    return pl.pallas_call(
        _inproj_kernel,
        grid=(r // tm, ncols),
        in_specs=[pl.BlockSpec((tm, d), lambda i, j: (i, 0)),
                  pl.BlockSpec((1, 1, d), sh_map),
                  pl.BlockSpec((1, 1, d), sc_map),
                  pl.BlockSpec((1, d), lambda i, j: (0, 0)),
                  pl.BlockSpec((d, tn), lambda i, j: (0, j))],
        out_specs=[pl.BlockSpec((tm, tn), lambda i, j: (i, jnp.minimum(j, ncols - 2))),
                   pl.BlockSpec((tm, tn), lambda i, j: (i, 0))],
        out_shape=[jax.ShapeDtypeStruct((r, (ncols - 1) * tn), BF16),
                   jax.ShapeDtypeStruct((r, tn), F32)],
        scratch_shapes=[pltpu.VMEM((tm, d), BF16)],
        name="inproj",
        compiler_params=_params(("arbitrary", "arbitrary")),
    )(x2d, mod3, mod3, g, w16)


def _pool_kernel(p_ref, w_ref, sc_ref, o_ref, pad_scr, tmp_scr):
    gw, pp = GRID_W, POOL_PAD
    pad_scr[...] = jnp.zeros(pad_scr.shape, F32)
    r_i = lax.broadcasted_iota(I32, (gw, gw, GROUP), 0)
    c_i = lax.broadcasted_iota(I32, (gw, gw, GROUP), 1)
    for g, win in enumerate(POOL_WINDOWS):
        half = win // 2
        pg = p_ref[:, g * GROUP:(g + 1) * GROUP].astype(F32).reshape(gw, gw, GROUP)
        pad_scr[pp:pp + gw, pp:pp + gw, :] = pg
        acc = None
        for k in range(-half, win - half):
            sl = pad_scr[pp + k:pp + k + gw, :, :]
            acc = sl if acc is None else acc + sl
        tmp_scr[...] = acc
        acc = None
        for k in range(-half, win - half):
            sl = tmp_scr[:, pp + k:pp + k + gw, :]
            acc = sl if acc is None else acc + sl
        cr = jnp.minimum(r_i - half + win, gw) - jnp.maximum(r_i - half, 0)
        cc = jnp.minimum(c_i - half + win, gw) - jnp.maximum(c_i - half, 0)
        mean = acc / (cr * cc).astype(F32)
        d = (mean - pg).reshape(gw * gw, GROUP)
        y = _dot(d.astype(BF16), w_ref[g].astype(BF16)) * sc_ref[:, g * GROUP:(g + 1) * GROUP]
        o_ref[:, g * GROUP:(g + 1) * GROUP] = y.astype(BF16)


def _pool_call(z, pool_w, pool_scale, batch, seq):
    width = len(POOL_WINDOWS) * GROUP
    ext = GRID_W + 2 * POOL_PAD
    return pl.pallas_call(
        _pool_kernel,
        grid=(batch,),
        in_specs=[pl.BlockSpec((seq, width), lambda b: (b, 0)),
                  pl.BlockSpec(pool_w.shape, lambda b: (0, 0, 0)),
                  pl.BlockSpec((1, width), lambda b: (0, 0))],
        out_specs=pl.BlockSpec((seq, width), lambda b: (b, 0)),
        out_shape=jax.ShapeDtypeStruct((batch * seq, width), BF16),
        scratch_shapes=[pltpu.VMEM((ext, ext, GROUP), F32), pltpu.VMEM((GRID_W, ext, GROUP), F32)],
        name="pool",
        compiler_params=_params(("arbitrary",)),
    )(z, pool_w, pool_scale)


def _hgrn_masks(rev):
    t, c = HG_BLOCK, HG_CHUNK
    row = lax.broadcasted_iota(I32, (t, t), 0)
    col = lax.broadcasted_iota(I32, (t, t), 1)
    sh = c.bit_length() - 1
    same = (row >> sh) == (col >> sh)
    tri = (col >= row) if rev else (col <= row)
    mid = ((row >> sh) << sh) + HG_MID
    to_mid = (col >= mid) if rev else (col <= mid)
    m_all = jnp.concatenate([jnp.where(same & tri, 1.0, 0.0),
                             jnp.where(same, 1.0, 0.0),
                             jnp.where(same & to_mid, 1.0, 0.0)], axis=0).astype(BF16)
    return m_all, same & tri, col >> sh


def _hgrn_blocks(chains, masks, emit_out):
    t, c = HG_BLOCK, HG_CHUNK
    nch = t // c
    nc = len(chains)
    revs = [ch[0] for ch in chains]
    qs = [ch[2] for ch in chains]
    vs = [ch[3] for ch in chains]
    sts = [ch[5] for ch in chains]
    ks, bs = [], []
    for rev, x_f, _, _, lb, _ in chains:
        f = lb + (1.0 - lb) * _sigmoid(x_f)
        g_hi, g_lo = _hi_lo(jnp.log(f))
        m_all = masks[rev][0]
        ks.append(1.0 - f)
        bs.append(_dot(m_all, g_hi) + _dot(m_all, g_lo))
    b_loc = [b[0:t] for b in bs]
    b_tot = [b[t:2 * t] for b in bs]
    b_mid = [b[2 * t:3 * t] for b in bs]
    kd16 = [(ks[i] * jnp.exp(b_tot[i] - b_loc[i])).astype(BF16) for i in range(nc)]
    vt = [v.T for v in vs]
    a_all = [jnp.exp(b) for b in b_tot]
    os_ = [None] * nc
    if emit_out:
        a16 = [(qs[i] * jnp.exp(b_loc[i])).astype(BF16) for i in range(nc)]
        qd16 = [(qs[i] * jnp.exp(b_loc[i] - b_mid[i])).astype(BF16) for i in range(nc)]
        kg16 = [(ks[i] * jnp.exp(b_mid[i] - b_loc[i])).astype(BF16) for i in range(nc)]
        attn = [jnp.where(masks[revs[i]][1], _dot_nt(qd16[i], kg16[i]), 0.0) for i in range(nc)]
        os_ = [_dot(attn[i].astype(BF16), vs[i].astype(BF16)) for i in range(nc)]
        o_inter = [[None] * nch for _ in range(nc)]
    for step in range(nch):
        for i in range(nc):
            n = nch - 1 - step if revs[i] else step
            if emit_out:
                o_inter[i][n] = _dot_nt(a16[i][n * c:(n + 1) * c], sts[i].astype(BF16))
            ut = _dot(jnp.where(masks[revs[i]][2] == n, vt[i], 0.0).astype(BF16), kd16[i])
            sts[i] = sts[i] * a_all[i][n * c:n * c + 1, :] + ut
    if emit_out:
        os_ = [os_[i] + jnp.concatenate(o_inter[i], axis=0) for i in range(nc)]
    return list(zip(os_, sts))


def _lower_bounds(lbl_ref):
    out = []
    for d in range(2):
        l0, l1 = lbl_ref[2 * d:2 * d + 1, :], lbl_ref[2 * d + 1:2 * d + 2, :]
        m = jnp.maximum(l0, l1)
        e0, e1 = jnp.exp(l0 - m), jnp.exp(l1 - m)
        out.append(e0 / (e0 + e1))
    return out


def _hgrn_run(nheads, seq, lbl_ref, f_refs, q_ref, v_ref, st_scr, o_scr):
    t = HG_BLOCK
    nblk = seq // t
    masks = (_hgrn_masks(False), _hgrn_masks(True))
    lbs = _lower_bounds(lbl_ref)
    emit_out = o_scr is not None

    def body(i, carry, accumulate):
        chains = []
        for d in range(2):
            blk = (nblk - 1 - i) if d else i
            rows = pl.ds(pl.multiple_of(blk * t, t), t)
            for h in range(nheads):
                ls = slice(h * GROUP, (h + 1) * GROUP)
                q = q_ref[rows, ls].astype(F32) if emit_out else None
                chains.append((d, h, rows, ls, f_refs[d][rows, ls], q, v_ref[rows, ls].astype(F32),
                               st_scr[d * nheads + h]))
        results = _hgrn_blocks([(d, x_f, q, v, lbs[d][:, ls], st) for d, h, rows, ls, x_f, q, v, st in chains],
                               masks, emit_out)
        for (d, h, rows, ls, *_), (o, st) in zip(chains, results):
            st_scr[d * nheads + h] = st
            if emit_out and accumulate:
                o_scr[rows, ls] += o
            elif emit_out:
                o_scr[rows, ls] = o
        return carry

    if emit_out:
        assert nblk % 2 == 0
        lax.fori_loop(0, nblk // 2, functools.partial(body, accumulate=False), 0)
        lax.fori_loop(nblk // 2, nblk, functools.partial(body, accumulate=True), 0)
    else:
        lax.fori_loop(0, nblk, functools.partial(body, accumulate=False), 0)


def _hgrn_ctx_kernel(v_ref, ff_ref, fb_ref, lbl_ref, sf_ref, sb_ref, st_scr, *, seq):
    nheads = sf_ref.shape[1]
    st_scr[...] = jnp.zeros(st_scr.shape, F32)
    _hgrn_run(nheads, seq, lbl_ref, (ff_ref, fb_ref), None, v_ref, st_scr, None)
    sf_ref[0] = st_scr[0:nheads]
    sb_ref[0] = st_scr[nheads:2 * nheads]


def _hgrn_kernel(q_ref, v_ref, ff_ref, fb_ref, og_ref, lbl_ref, ng_ref, sf_ref, sb_ref, y_ref,
                 st_scr, o_scr, *, seq):
    nheads = sf_ref.shape[1]
    st_scr[0:nheads] = sf_ref[0]
    st_scr[nheads:2 * nheads] = sb_ref[0]
    _hgrn_run(nheads, seq, lbl_ref, (ff_ref, fb_ref), q_ref, v_ref, st_scr, o_scr)
    rb = 4 * HG_BLOCK

    def epilogue(i, carry):
        rows = pl.ds(pl.multiple_of(i * rb, rb), rb)
        for h in range(nheads):
            ls = slice(h * GROUP, (h + 1) * GROUP)
            og = og_ref[rows, ls].astype(F32)
            y = _rms(o_scr[rows, ls]) * ng_ref[...] * (og * _sigmoid(og))
            y_ref[rows, ls] = y.astype(BF16)
        return carry

    lax.fori_loop(0, seq // rb, epilogue, 0)


def _hgrn_specs(seq):
    hw = HG_HEADS * GROUP
    big = lambda k: pl.BlockSpec((seq, hw), lambda b: (b, k), pipeline_mode=pl.Buffered(1))
    st_spec = pl.BlockSpec((1, HG_HEADS, GROUP, GROUP), lambda b: (b, 0, 0, 0))
    return hw, big, st_spec


def _hgrn_ctx_call(zc16, zcf, lbl, batch, seq):
    hw, big, st_spec = _hgrn_specs(seq)
    st_shape = jax.ShapeDtypeStruct((batch, HG_HEADS, GROUP, GROUP), F32)
    return pl.pallas_call(
        functools.partial(_hgrn_ctx_kernel, seq=seq),
        grid=(batch,),
        in_specs=[big(2), big(0), big(1), pl.BlockSpec((4, hw), lambda b: (0, 0))],
        out_specs=[st_spec, st_spec],
        out_shape=[st_shape, st_shape],
        scratch_shapes=[pltpu.VMEM((2 * HG_HEADS, GROUP, GROUP), F32)],
        name="hgrn_ctx",
        compiler_params=_params(("arbitrary",)),
    )(zc16, zcf, zcf, lbl)


def _hgrn_call(z16, zf, lbl, norm_g, sf, sb, batch, seq):
    hw, big, st_spec = _hgrn_specs(seq)
    return pl.pallas_call(
        functools.partial(_hgrn_kernel, seq=seq),
        grid=(batch,),
        in_specs=[big(1), big(2), big(0), big(1), big(3),
                  pl.BlockSpec((4, hw), lambda b: (0, 0)),
                  pl.BlockSpec((1, GROUP), lambda b: (0, 0)),
                  st_spec, st_spec],
        out_specs=pl.BlockSpec((seq, hw), lambda b: (b, 0)),
        out_shape=jax.ShapeDtypeStruct((batch * seq, hw), BF16),
        scratch_shapes=[pltpu.VMEM((2 * HG_HEADS, GROUP, GROUP), F32), pltpu.VMEM((seq, hw), F32)],
        name="hgrn",
        compiler_params=_params(("arbitrary",)),
    )(z16, z16, zf, zf, z16, lbl, norm_g, sf, sb)


def _merge_kernel(x_ref, yp_ref, yh_ref, gp_ref, gh_ref, g1_ref, sh_ref, sc_ref, ng_ref,
                  wa_ref, wb_ref, wo_ref, wqt_ref, keys_ref, x1_ref, h2_ref, st_ref):
    m = (_sigmoid(gp_ref[...].astype(F32)) * _dot(yp_ref[...], wa_ref[...])
         + _sigmoid(gh_ref[...].astype(F32)) * _dot(yh_ref[...], wb_ref[...]))
    x1 = x_ref[...] + g1_ref[0] * _dot(m.astype(BF16), wo_ref[...])
    x1_ref[...] = x1
    h2 = _rms(x1) * ng_ref[...] * (1.0 + sc_ref[0]) + sh_ref[0]
    h2_ref[...] = (h2 * _SQRT_HALF).astype(BF16)
    qt = _dot(wqt_ref[...], h2.T.astype(BF16))
    for hp in range(2 * PEER_HEADS):
        st_ref[hp] = _dot(keys_ref[hp], qt[hp * GROUP:(hp + 1) * GROUP].astype(BF16))


def _merge_call(x2d, ypool, yhg, z, mod3, norm2_g, wa16, wb16, wo16, wqt16, keys16, seq, tm):
    r, d = x2d.shape
    gcol0 = (z.shape[1] - 2 * d) // d
    modrow = lambda k: (lambda i: (k + N_MOD * ((i * tm) // seq), 0, 0))
    const2 = lambda i: (0, 0)
    nhp = 2 * PEER_HEADS
    return pl.pallas_call(
        _merge_kernel,
        grid=(r // tm,),
        in_specs=[pl.BlockSpec((tm, d), lambda i: (i, 0)),
                  pl.BlockSpec((tm, ypool.shape[1]), lambda i: (i, 0)),
                  pl.BlockSpec((tm, yhg.shape[1]), lambda i: (i, 0)),
                  pl.BlockSpec((tm, d), lambda i: (i, gcol0)),
                  pl.BlockSpec((tm, d), lambda i: (i, gcol0 + 1)),
                  pl.BlockSpec((1, 1, d), modrow(2)),
                  pl.BlockSpec((1, 1, d), modrow(3)),
                  pl.BlockSpec((1, 1, d), modrow(4)),
                  pl.BlockSpec((1, d), const2),
                  pl.BlockSpec(wa16.shape, const2),
                  pl.BlockSpec(wb16.shape, const2),
                  pl.BlockSpec(wo16.shape, const2),
                  pl.BlockSpec(wqt16.shape, const2),
                  pl.BlockSpec(keys16.shape, lambda i: (0, 0, 0))],
        out_specs=[pl.BlockSpec((tm, d), lambda i: (i, 0)),
                   pl.BlockSpec((tm, d), lambda i: (i, 0)),
                   pl.BlockSpec((nhp, PEER_NKEYS, tm), lambda i: (0, 0, i))],
        out_shape=[jax.ShapeDtypeStruct((r, d), F32),
                   jax.ShapeDtypeStruct((r, d), BF16),
                   jax.ShapeDtypeStruct((nhp, PEER_NKEYS, r), F32)],
        name="merge",
        compiler_params=_params(("arbitrary",)),
    )(x2d, ypool, yhg, z, z, mod3, mod3, mod3, norm2_g, wa16, wb16, wo16, wqt16, keys16)


def _top16(v, iota_n):
    vals, idxs = [], []
    for _ in range(PEER_TOPK):
        m = jnp.max(v, axis=0, keepdims=True)
        idx = jnp.min(jnp.where(v == m, iota_n, float(PEER_NKEYS)), axis=0, keepdims=True)
        v = jnp.where(iota_n == idx, -jnp.inf, v)
        vals.append(m)
        idxs.append(idx)
    return jnp.concatenate(vals, axis=0), jnp.concatenate(idxs, axis=0)


def _topk_unit(s1, s2):
    tg = s1.shape[1]
    kk = PEER_TOPK
    nkf = float(PEER_NKEYS)
    iota_n = lax.broadcasted_iota(I32, (PEER_NKEYS, tg), 0).astype(F32)
    v1, n1 = _top16(s1, iota_n)
    v2, n2 = _top16(s2, iota_n)

    r = lax.broadcasted_iota(I32, (8, tg), 0)
    rf = r.astype(F32)
    sel3 = lambda x: jnp.where(r < 5, x[0], x[1])
    sel4 = lambda x: jnp.where(r < 4, x[0], jnp.where(r < 6, x[1], x[2]))
    sel5 = lambda x: jnp.where(r < 2, x[0], x[1])

    def tiles(l1, l2):
        row1 = lambda a: l1[a:a + 1]
        lo2, hi1 = l2[0:8], l1[8:16]
        hi1_r2 = pltpu.roll(hi1, 2, 0)
        t1 = [row1(0), row1(0), row1(1), sel3((row1(2), row1(4))), sel4((row1(3), row1(5), row1(6))),
              sel5((row1(7), hi1_r2)), hi1_r2]
        t2 = [lo2, l2[8:16], lo2, sel3((lo2, pltpu.roll(lo2, 5, 0))),
              sel4((lo2, pltpu.roll(lo2, 4, 0), pltpu.roll(lo2, 6, 0))), sel5((lo2, l2[0:1])), l2[0:1]]
        return t1, t2

    tv1, tv2 = tiles(v1, v2)
    tn1, tn2 = tiles(n1, n2)
    cand = [a + b for a, b in zip(tv1, tv2)]
    cand[6] = jnp.where(r < 2, cand[6], -jnp.inf)
    cand = jnp.concatenate(cand, axis=0)
    eid = jnp.concatenate([a * nkf + b for a, b in zip(tn1, tn2)], axis=0)
    pos = jnp.concatenate([rf, rf + 8.0, rf + 16.0, sel3((rf + 32.0, rf + 59.0)),
                           sel4((rf + 48.0, rf + 76.0, rf + 90.0)), sel5((rf + 112.0, rf * 16.0 + 96.0)),
                           rf * 16.0 + 224.0], axis=0)
    tops, es = [], []
    for _ in range(kk):
        m = jnp.max(cand, axis=0, keepdims=True)
        p = jnp.min(jnp.where(cand == m, pos, 1e9), axis=0, keepdims=True)
        sel = pos == p
        es.append(jnp.max(jnp.where(sel, eid, -1.0), axis=0, keepdims=True))
        cand = jnp.where(sel, -jnp.inf, cand)
        tops.append(m)
    top = jnp.concatenate(tops, axis=0)
    e = jnp.concatenate(es, axis=0)
    ex = jnp.exp(top - top[0:1])
    gate = ex / jnp.sum(ex, axis=0, keepdims=True)
    e1 = jnp.floor(e * (1.0 / PEER_NKEYS))
    return e1, e - e1 * nkf, gate * _SQRT_HALF


def _peer_kernel(s_ref, h2_ref, ut_ref, v_ref, x1_ref, g2_ref, fg_ref, o_ref,
                 w_scr, stage_scr, tr0_scr, tr1_scr, acc_scr, rt_scr, rr_scr):
    tb, d = h2_ref.shape
    eb = v_ref.shape[0]
    i = pl.program_id(0)
    e = pl.program_id(1)
    nk = PEER_NKEYS
    ngrp = tb // GROUP
    pack = w_scr.shape[2]
    unroll = stage_scr.shape[0]
    tr_scrs = (tr0_scr, tr1_scr)
    units = s_ref.shape[0] // 2
    steps_per_grp = PEER_HEADS // units

    @pl.when((i == 0) & (e == 0))
    def _init():
        rt_scr[...] = jnp.zeros(rt_scr.shape, F32)

    @pl.when(e == 0)
    def _build():
        rd = lax.rem(i + 1, 2)
        for fld in range(3):
            for g in range(ngrp):
                rr_scr[fld, g * GROUP:(g + 1) * GROUP, :] = rt_scr[rd, fld, g].T
        iota_s = lax.broadcasted_iota(I32, (nk, nk), 0).astype(F32).astype(BF16)
        zero = jnp.zeros((), BF16)
        one = jnp.ones((), BF16)

        def flush(tt):
            for g in range(unroll // pack):
                blk = stage_scr[g * pack:(g + 1) * pack].astype(BF16)
                tr_scrs[g][...] = jnp.transpose(blk, (1, 0, 2))
            for g in range(unroll // pack):
                w_scr[tt * (unroll // pack) + g] = tr_scrs[g][...]

        def body(tt, carry):
            flush(jnp.maximum(tt - 1, 0))
            ts = [tt * unroll + r for r in range(unroll)]
            row = lambda fld, t: jnp.broadcast_to(rr_scr[fld, pl.ds(t, 1), :], (pack, nk)).astype(BF16)
            tiles = [iota_s[v * pack:(v + 1) * pack] for v in range(nk // pack)]
            onehot = lambda idx, val: jnp.concatenate([jnp.where(tl == idx, val, zero) for tl in tiles], axis=0)
            g1t = [onehot(row(0, t), row(2, t)) for t in ts]
            p2t = [onehot(row(1, t), one) for t in ts]
            for r, (g, p) in enumerate(zip(g1t, p2t)):
                stage_scr[r] = _dot_nt(g, p)
            return carry

        stage_scr[...] = jnp.zeros(stage_scr.shape, F32)
        ntrip = tb // unroll
        lax.fori_loop(0, ntrip, body, 0)
        flush(ntrip - 1)
        acc_scr[...] = jnp.zeros(acc_scr.shape, F32)

    grp = e // steps_per_grp
    head0 = lax.rem(e, steps_per_grp) * units
    wr = lax.rem(i, 2)
    n1_0 = e * (eb // nk)
    nchunk = eb // (2 * nk)
    pieces = []

    def route(k):
        i1, i2, gate = _topk_unit(s_ref[2 * k], s_ref[2 * k + 1])
        rows = pl.ds(pl.multiple_of((head0 + k) * PEER_TOPK, PEER_TOPK), PEER_TOPK)
        rt_scr[wr, 0, grp, rows, :] = i1
        rt_scr[wr, 1, grp, rows, :] = i2
        rt_scr[wr, 2, grp, rows, :] = gate

    def activate(k2):
        a2 = _dot(h2_ref[...], ut_ref[:, 2 * k2 * nk:2 * (k2 + 1) * nk])
        for k in (2 * k2, 2 * k2 + 1):
            ak = a2[:, (k - 2 * k2) * nk:(k - 2 * k2 + 1) * nk]
            wn = w_scr[:, n1_0 + k, :, :].reshape(tb, nk)
            act = ak * (1.0 + lax.erf(ak))
            pieces.append(wn * act.astype(BF16))

    per = nchunk // units
    for k in range(units):
        route(k)
        for k2 in range(k * per, (k + 1) * per):
            activate(k2)
    wg = jnp.concatenate(pieces, axis=1)
    tn = 256
    for j in range(d // tn):
        acc_scr[:, j * tn:(j + 1) * tn] += _dot(wg, v_ref[:, j * tn:(j + 1) * tn])

    @pl.when(e == pl.num_programs(1) - 1)
    def _fin():
        x2 = x1_ref[...] + g2_ref[0] * acc_scr[...]
        o_ref[...] = _rms(x2) * fg_ref[...]


def _peer_call(st, h2, ut16, v16, x1, mod3, final_g, seq, tb, eb):
    r, d = h2.shape
    ne = v16.shape[0]
    nblk = r // tb
    ngrp = tb // GROUP
    nsteps = ne // eb
    units = PEER_HEADS * ngrp // nsteps
    assert units * nsteps == PEER_HEADS * ngrp and PEER_HEADS % units == 0
    steps_per_grp = PEER_HEADS // units
    slots = PEER_HEADS * PEER_TOPK
    pack = 16
    prev = lambda i: jnp.maximum(i - 1, 0)
    tok = lambda i, e: (prev(i), 0)
    return pl.pallas_call(
        _peer_kernel,
        grid=(nblk + 1, nsteps),
        in_specs=[pl.BlockSpec((2 * units, PEER_NKEYS, GROUP),
                               lambda i, e: (lax.rem(e, steps_per_grp), 0,
                                             jnp.minimum(i, nblk - 1) * ngrp + e // steps_per_grp)),
                  pl.BlockSpec((tb, d), tok),
                  pl.BlockSpec((d, eb), lambda i, e: (0, e)),
                  pl.BlockSpec((eb, d), lambda i, e: (e, 0)),
                  pl.BlockSpec((tb, d), tok, pipeline_mode=pl.Buffered(1)),
                  pl.BlockSpec((1, 1, d), lambda i, e: (5 + N_MOD * ((prev(i) * tb) // seq), 0, 0)),
                  pl.BlockSpec((1, d), lambda i, e: (0, 0))],
        out_specs=pl.BlockSpec((tb, d), tok),
        out_shape=jax.ShapeDtypeStruct((r, d), F32),
        scratch_shapes=[pltpu.VMEM((tb // pack, PEER_NKEYS, pack, PEER_NKEYS), BF16),
                        pltpu.VMEM((2 * pack, PEER_NKEYS, PEER_NKEYS), F32),
                        pltpu.VMEM((PEER_NKEYS, pack, PEER_NKEYS), BF16),
                        pltpu.VMEM((PEER_NKEYS, pack, PEER_NKEYS), BF16),
                        pltpu.VMEM((tb, d), F32),
                        pltpu.VMEM((2, 3, ngrp, slots, GROUP), F32),
                        pltpu.VMEM((3, tb, slots), F32)],
        name="peer",
        compiler_params=_params(("arbitrary", "arbitrary")),
    )(st, h2, ut16, v16, x1, mod3, final_g)


def kernel(x, c, ctx, c_ctx, w_mod, b_mod, norm1_g, norm2_g, w_in, pool_w, pool_scale, hg_lb_logits, hg_norm_g,
           w_a, w_b, w_o, peer_wq, peer_keys, peer_u, peer_v, final_g):
    batch, seq, d = x.shape
    ctx_len = ctx.shape[1]
    assert w_mod.shape[0] == 1 and seq == GRID_W * GRID_W and hg_lb_logits.shape[1] == 2
    hw = HG_HEADS * GROUP

    cvec = jnp.concatenate([c, c_ctx[None, :], jnp.zeros((8 - batch - 1, d), F32)], axis=0)
    mod = _mod_call(cvec, w_mod[0], b_mod[0][None, :])
    mod3 = mod.reshape(8 * N_MOD, 1, d)

    w_in16 = w_in[0].astype(BF16)
    f0, f1 = 3 * hw, 5 * hw
    w_in16 = jnp.concatenate([w_in16[:, :f0], w_in16[:, f1:], w_in16[:, f0:f1]], axis=1)
    g1n = norm1_g[0][None, :]
    x2d = x.reshape(batch * seq, d)
    z16, zf = _inproj_call(x2d, mod3, 0, 1, seq, g1n, w_in16, 1024, 2 * hw)
    zc16, zcf = _inproj_call(ctx.reshape(batch * ctx_len, d), mod3, N_MOD * batch, N_MOD * batch + 1, None, g1n,
                             w_in16, 256, 2 * hw)

    lbl = hg_lb_logits.reshape(2 * hg_lb_logits.shape[1], hw)
    sf, sb = _hgrn_ctx_call(zc16, zcf, lbl, batch, ctx_len)
    yhg = _hgrn_call(z16, zf, lbl, hg_norm_g[0][None, :], sf, sb, batch, seq)
    ypool = _pool_call(z16, pool_w[0], pool_scale[0][None, :], batch, seq)

    nhp = 2 * PEER_HEADS
    x1, h2, st = _merge_call(x2d, ypool, yhg, z16, mod3, norm2_g[0][None, :],
                             w_a[0].astype(BF16), w_b[0].astype(BF16), w_o[0].astype(BF16),
                             peer_wq[0].T.astype(BF16),
                             peer_keys[0].reshape(nhp, PEER_NKEYS, GROUP).astype(BF16), seq, 256)
    out = _peer_call(st, h2, peer_u[0].T.astype(BF16), peer_v[0].astype(BF16), x1, mod3,
                     final_g[None, :], seq, 512, 2048)
    return out.reshape(batch, seq, d)
```

```python
import functools
import math

import jax
import jax.numpy as jnp
from jax import lax
from jax.experimental import pallas as pl
from jax.experimental.pallas import tpu as pltpu

F32 = jnp.float32
BF16 = jnp.bfloat16
I32 = jnp.int32

EPS = 1e-6
GRID_W = 64
POOL_WINDOWS = (2, 4, 8, 16)
POOL_PAD = 8
GROUP = 128
HG_HEADS = 4
HG_CHUNK = 16
HG_BLOCK = 128
HG_MID = 8
PEER_HEADS = 8
PEER_NKEYS = 128
PEER_TOPK = 16
N_MOD = 6
VMEM_LIMIT = 60 * 1024 * 1024

_NT = (((1,), (1,)), ((), ()))
_SQRT_HALF = math.sqrt(0.5)


def _dot(a, b):
    return jnp.dot(a, b, preferred_element_type=F32)


def _dot_nt(a, b):
    return lax.dot_general(a, b, _NT, preferred_element_type=F32)


def _hi_lo(a):
    hi = a.astype(BF16)
    return hi, (a - hi.astype(F32)).astype(BF16)


def _sigmoid(x):
    return 1.0 / (1.0 + jnp.exp(-x))


def _rms(x):
    return x * lax.rsqrt(jnp.mean(x * x, axis=-1, keepdims=True) + EPS)


def _params(sem):
    return pltpu.CompilerParams(dimension_semantics=sem, vmem_limit_bytes=VMEM_LIMIT)


def _mod_kernel(c_ref, w_ref, b_ref, o_ref):
    c = c_ref[...]
    s = c * _sigmoid(c)
    s_hi, s_lo = _hi_lo(s)
    w_hi, w_lo = _hi_lo(w_ref[...])
    o_ref[...] = _dot(s_hi, w_hi) + (_dot(s_hi, w_lo) + _dot(s_lo, w_hi)) + b_ref[...]


def _mod_call(cvec, w_mod, b_mod):
    rows, d = cvec.shape
    n = w_mod.shape[1]
    tn = n // 4
    return pl.pallas_call(
        _mod_kernel,
        grid=(n // tn,),
        in_specs=[pl.BlockSpec((rows, d), lambda j: (0, 0)),
                  pl.BlockSpec((d, tn), lambda j: (0, j)),
                  pl.BlockSpec((1, tn), lambda j: (0, j))],
        out_specs=pl.BlockSpec((rows, tn), lambda j: (0, j)),
        out_shape=jax.ShapeDtypeStruct((rows, n), F32),
        name="mod",
        compiler_params=_params(("arbitrary",)),
    )(cvec, w_mod, b_mod)


def _inproj_kernel(x_ref, sh_ref, sc_ref, g_ref, w_ref, o16_ref, of_ref, h_scr):
    j = pl.program_id(1)
    last = pl.num_programs(1) - 1

    @pl.when(j == 0)
    def _():
        h = _rms(x_ref[...]) * g_ref[...] * (1.0 + sc_ref[0]) + sh_ref[0]
        h_scr[...] = h.astype(BF16)

    y = _dot(h_scr[...], w_ref[...])

    @pl.when(j < last)
    def _():
        o16_ref[...] = y.astype(BF16)

    @pl.when(j == last)
    def _():
        of_ref[...] = y


def _inproj_call(x2d, mod3, sh_row, sc_row, rows_per_mod, g, w16, tm, tn):
    r, d = x2d.shape
    ncols = w16.shape[1] // tn
    if rows_per_mod is None:
        sh_map = lambda i, j: (sh_row, 0, 0)
        sc_map = lambda i, j: (sc_row, 0, 0)
    else:
        sh_map = lambda i, j: (sh_row + N_MOD * ((i * tm) // rows_per_mod), 0, 0)
        sc_map = lambda i, j: (sc_row + N_MOD * ((i * tm) // rows_per_mod), 0, 0)
    return pl.pallas_call(
        _inproj_kernel,
        grid=(r // tm, ncols),
        in_specs=[pl.BlockSpec((tm, d), lambda i, j: (i, 0)),
                  pl.BlockSpec((1, 1, d), sh_map),
                  pl.BlockSpec((1, 1, d), sc_map),
                  pl.BlockSpec((1, d), lambda i, j: (0, 0)),
                  pl.BlockSpec((d, tn), lambda i, j: (0, j))],
        out_specs=[pl.BlockSpec((tm, tn), lambda i, j: (i, jnp.minimum(j, ncols - 2))),
                   pl.BlockSpec((tm, tn), lambda i, j: (i, 0))],
        out_shape=[jax.ShapeDtypeStruct((r, (ncols - 1) * tn), BF16),
                   jax.ShapeDtypeStruct((r, tn), F32)],
        scratch_shapes=[pltpu.VMEM((tm, d), BF16)],
        name="inproj",
        compiler_params=_params(("arbitrary", "arbitrary")),
    )(x2d, mod3, mod3, g, w16)


def _pool_kernel(p_ref, w_ref, sc_ref, o_ref, pad_scr, tmp_scr):
    gw, pp = GRID_W, POOL_PAD
    pad_scr[...] = jnp.zeros(pad_scr.shape, F32)
    r_i = lax.broadcasted_iota(I32, (gw, gw, GROUP), 0)
    c_i = lax.broadcasted_iota(I32, (gw, gw, GROUP), 1)
    for g, win in enumerate(POOL_WINDOWS):
        half = win // 2
        pg = p_ref[:, g * GROUP:(g + 1) * GROUP].astype(F32).reshape(gw, gw, GROUP)
        pad_scr[pp:pp + gw, pp:pp + gw, :] = pg
        acc = None
        for k in range(-half, win - half):
            sl = pad_scr[pp + k:pp + k + gw, :, :]
            acc = sl if acc is None else acc + sl
        tmp_scr[...] = acc
        acc = None
        for k in range(-half, win - half):
            sl = tmp_scr[:, pp + k:pp + k + gw, :]
            acc = sl if acc is None else acc + sl
        cr = jnp.minimum(r_i - half + win, gw) - jnp.maximum(r_i - half, 0)
        cc = jnp.minimum(c_i - half + win, gw) - jnp.maximum(c_i - half, 0)
        mean = acc / (cr * cc).astype(F32)
        d = (mean - pg).reshape(gw * gw, GROUP)
        y = _dot(d.astype(BF16), w_ref[g].astype(BF16)) * sc_ref[:, g * GROUP:(g + 1) * GROUP]
        o_ref[:, g * GROUP:(g + 1) * GROUP] = y.astype(BF16)


def _pool_call(z, pool_w, pool_scale, batch, seq):
    width = len(POOL_WINDOWS) * GROUP
    ext = GRID_W + 2 * POOL_PAD
    return pl.pallas_call(
        _pool_kernel,
        grid=(batch,),
        in_specs=[pl.BlockSpec((seq, width), lambda b: (b, 0)),
                  pl.BlockSpec(pool_w.shape, lambda b: (0, 0, 0)),
                  pl.BlockSpec((1, width), lambda b: (0, 0))],
        out_specs=pl.BlockSpec((seq, width), lambda b: (b, 0)),
        out_shape=jax.ShapeDtypeStruct((batch * seq, width), BF16),
        scratch_shapes=[pltpu.VMEM((ext, ext, GROUP), F32), pltpu.VMEM((GRID_W, ext, GROUP), F32)],
        name="pool",
        compiler_params=_params(("arbitrary",)),
    )(z, pool_w, pool_scale)


def _hgrn_masks(rev):
    t, c = HG_BLOCK, HG_CHUNK
    row = lax.broadcasted_iota(I32, (t, t), 0)
    col = lax.broadcasted_iota(I32, (t, t), 1)
    sh = c.bit_length() - 1
    same = (row >> sh) == (col >> sh)
    tri = (col >= row) if rev else (col <= row)
    mid = ((row >> sh) << sh) + HG_MID
    to_mid = (col >= mid) if rev else (col <= mid)
    m_all = jnp.concatenate([jnp.where(same & tri, 1.0, 0.0),
                             jnp.where(same, 1.0, 0.0),
                             jnp.where(same & to_mid, 1.0, 0.0)], axis=0).astype(BF16)
    return m_all, same & tri, col >> sh


def _hgrn_blocks(chains, masks, emit_out):
    t, c = HG_BLOCK, HG_CHUNK
    nch = t // c
    nc = len(chains)
    revs = [ch[0] for ch in chains]
    qs = [ch[2] for ch in chains]
    vs = [ch[3] for ch in chains]
    sts = [ch[5] for ch in chains]
    ks, bs = [], []
    for rev, x_f, _, _, lb, _ in chains:
        f = lb + (1.0 - lb) * _sigmoid(x_f)
        g_hi, g_lo = _hi_lo(jnp.log(f))
        m_all = masks[rev][0]
        ks.append(1.0 - f)
        bs.append(_dot(m_all, g_hi) + _dot(m_all, g_lo))
    b_loc = [b[0:t] for b in bs]
    b_tot = [b[t:2 * t] for b in bs]
    b_mid = [b[2 * t:3 * t] for b in bs]
    kd16 = [(ks[i] * jnp.exp(b_tot[i] - b_loc[i])).astype(BF16) for i in range(nc)]
    vt = [v.T for v in vs]
    a_all = [jnp.exp(b) for b in b_tot]
    os_ = [None] * nc
    if emit_out:
        a16 = [(qs[i] * jnp.exp(b_loc[i])).astype(BF16) for i in range(nc)]
        qd16 = [(qs[i] * jnp.exp(b_loc[i] - b_mid[i])).astype(BF16) for i in range(nc)]
        kg16 = [(ks[i] * jnp.exp(b_mid[i] - b_loc[i])).astype(BF16) for i in range(nc)]
        attn = [jnp.where(masks[revs[i]][1], _dot_nt(qd16[i], kg16[i]), 0.0) for i in range(nc)]
        os_ = [_dot(attn[i].astype(BF16), vs[i].astype(BF16)) for i in range(nc)]
        o_inter = [[None] * nch for _ in range(nc)]
    for step in range(nch):
        for i in range(nc):
            n = nch - 1 - step if revs[i] else step
            if emit_out:
                o_inter[i][n] = _dot_nt(a16[i][n * c:(n + 1) * c], sts[i].astype(BF16))
            ut = _dot(jnp.where(masks[revs[i]][2] == n, vt[i], 0.0).astype(BF16), kd16[i])
            sts[i] = sts[i] * a_all[i][n * c:n * c + 1, :] + ut
    if emit_out:
        os_ = [os_[i] + jnp.concatenate(o_inter[i], axis=0) for i in range(nc)]
    return list(zip(os_, sts))


def _lower_bounds(lbl_ref):
    out = []
    for d in range(2):
        l0, l1 = lbl_ref[2 * d:2 * d + 1, :], lbl_ref[2 * d + 1:2 * d + 2, :]
        m = jnp.maximum(l0, l1)
        e0, e1 = jnp.exp(l0 - m), jnp.exp(l1 - m)
        out.append(e0 / (e0 + e1))
    return out


def _hgrn_run(nheads, seq, lbl_ref, f_refs, q_ref, v_ref, st_scr, o_scr):
    t = HG_BLOCK
    nblk = seq // t
    masks = (_hgrn_masks(False), _hgrn_masks(True))
    lbs = _lower_bounds(lbl_ref)
    emit_out = o_scr is not None

    def body(i, carry, accumulate):
        chains = []
        for d in range(2):
            blk = (nblk - 1 - i) if d else i
            rows = pl.ds(pl.multiple_of(blk * t, t), t)
            for h in range(nheads):
                ls = slice(h * GROUP, (h + 1) * GROUP)
                q = q_ref[rows, ls].astype(F32) if emit_out else None
                chains.append((d, h, rows, ls, f_refs[d][rows, ls], q, v_ref[rows, ls].astype(F32),
                               st_scr[d * nheads + h]))
        results = _hgrn_blocks([(d, x_f, q, v, lbs[d][:, ls], st) for d, h, rows, ls, x_f, q, v, st in chains],
                               masks, emit_out)
        for (d, h, rows, ls, *_), (o, st) in zip(chains, results):
            st_scr[d * nheads + h] = st
            if emit_out and accumulate:
                o_scr[rows, ls] += o
            elif emit_out:
                o_scr[rows, ls] = o
        return carry

    if emit_out:
        assert nblk % 2 == 0
        lax.fori_loop(0, nblk // 2, functools.partial(body, accumulate=False), 0)
        lax.fori_loop(nblk // 2, nblk, functools.partial(body, accumulate=True), 0)
    else:
        lax.fori_loop(0, nblk, functools.partial(body, accumulate=False), 0)


def _hgrn_ctx_kernel(v_ref, ff_ref, fb_ref, lbl_ref, sf_ref, sb_ref, st_scr, *, seq):
    nheads = sf_ref.shape[1]
    st_scr[...] = jnp.zeros(st_scr.shape, F32)
    _hgrn_run(nheads, seq, lbl_ref, (ff_ref, fb_ref), None, v_ref, st_scr, None)
    sf_ref[0] = st_scr[0:nheads]
    sb_ref[0] = st_scr[nheads:2 * nheads]


def _hgrn_kernel(q_ref, v_ref, ff_ref, fb_ref, og_ref, lbl_ref, ng_ref, sf_ref, sb_ref, y_ref,
                 st_scr, o_scr, *, seq):
    nheads = sf_ref.shape[1]
    st_scr[0:nheads] = sf_ref[0]
    st_scr[nheads:2 * nheads] = sb_ref[0]
    _hgrn_run(nheads, seq, lbl_ref, (ff_ref, fb_ref), q_ref, v_ref, st_scr, o_scr)
    rb = 4 * HG_BLOCK

    def epilogue(i, carry):
        rows = pl.ds(pl.multiple_of(i * rb, rb), rb)
        for h in range(nheads):
            ls = slice(h * GROUP, (h + 1) * GROUP)
            og = og_ref[rows, ls].astype(F32)
            y = _rms(o_scr[rows, ls]) * ng_ref[...] * (og * _sigmoid(og))
            y_ref[rows, ls] = y.astype(BF16)
        return carry

    lax.fori_loop(0, seq // rb, epilogue, 0)


def _hgrn_specs(seq):
    hw = HG_HEADS * GROUP
    big = lambda k: pl.BlockSpec((seq, hw), lambda b: (b, k), pipeline_mode=pl.Buffered(1))
    st_spec = pl.BlockSpec((1, HG_HEADS, GROUP, GROUP), lambda b: (b, 0, 0, 0))
    return hw, big, st_spec


def _hgrn_ctx_call(zc16, zcf, lbl, batch, seq):
    hw, big, st_spec = _hgrn_specs(seq)
    st_shape = jax.ShapeDtypeStruct((batch, HG_HEADS, GROUP, GROUP), F32)
    return pl.pallas_call(
        functools.partial(_hgrn_ctx_kernel, seq=seq),
        grid=(batch,),
        in_specs=[big(2), big(0), big(1), pl.BlockSpec((4, hw), lambda b: (0, 0))],
        out_specs=[st_spec, st_spec],
        out_shape=[st_shape, st_shape],
        scratch_shapes=[pltpu.VMEM((2 * HG_HEADS, GROUP, GROUP), F32)],
        name="hgrn_ctx",
        compiler_params=_params(("arbitrary",)),
    )(zc16, zcf, zcf, lbl)


def _hgrn_call(z16, zf, lbl, norm_g, sf, sb, batch, seq):
    hw, big, st_spec = _hgrn_specs(seq)
    return pl.pallas_call(
        functools.partial(_hgrn_kernel, seq=seq),
        grid=(batch,),
        in_specs=[big(1), big(2), big(0), big(1), big(3),
                  pl.BlockSpec((4, hw), lambda b: (0, 0)),
                  pl.BlockSpec((1, GROUP), lambda b: (0, 0)),
                  st_spec, st_spec],
        out_specs=pl.BlockSpec((seq, hw), lambda b: (b, 0)),
        out_shape=jax.ShapeDtypeStruct((batch * seq, hw), BF16),
        scratch_shapes=[pltpu.VMEM((2 * HG_HEADS, GROUP, GROUP), F32), pltpu.VMEM((seq, hw), F32)],
        name="hgrn",
        compiler_params=_params(("arbitrary",)),
    )(z16, z16, zf, zf, z16, lbl, norm_g, sf, sb)


def _merge_kernel(x_ref, yp_ref, yh_ref, gp_ref, gh_ref, g1_ref, sh_ref, sc_ref, ng_ref,
                  wa_ref, wb_ref, wo_ref, wqt_ref, keys_ref, x1_ref, h2_ref, st_ref):
    m = (_sigmoid(gp_ref[...].astype(F32)) * _dot(yp_ref[...], wa_ref[...])
         + _sigmoid(gh_ref[...].astype(F32)) * _dot(yh_ref[...], wb_ref[...]))
    x1 = x_ref[...] + g1_ref[0] * _dot(m.astype(BF16), wo_ref[...])
    x1_ref[...] = x1
    h2 = _rms(x1) * ng_ref[...] * (1.0 + sc_ref[0]) + sh_ref[0]
    h2_ref[...] = (h2 * _SQRT_HALF).astype(BF16)
    qt = _dot(wqt_ref[...], h2.T.astype(BF16))
    for hp in range(2 * PEER_HEADS):
        st_ref[hp] = _dot(keys_ref[hp], qt[hp * GROUP:(hp + 1) * GROUP].astype(BF16))


def _merge_call(x2d, ypool, yhg, z, mod3, norm2_g, wa16, wb16, wo16, wqt16, keys16, seq, tm):
    r, d = x2d.shape
    gcol0 = (z.shape[1] - 2 * d) // d
    modrow = lambda k: (lambda i: (k + N_MOD * ((i * tm) // seq), 0, 0))
    const2 = lambda i: (0, 0)
    nhp = 2 * PEER_HEADS
    return pl.pallas_call(
        _merge_kernel,
        grid=(r // tm,),
        in_specs=[pl.BlockSpec((tm, d), lambda i: (i, 0)),
                  pl.BlockSpec((tm, ypool.shape[1]), lambda i: (i, 0)),
                  pl.BlockSpec((tm, yhg.shape[1]), lambda i: (i, 0)),
                  pl.BlockSpec((tm, d), lambda i: (i, gcol0)),
                  pl.BlockSpec((tm, d), lambda i: (i, gcol0 + 1)),
                  pl.BlockSpec((1, 1, d), modrow(2)),
                  pl.BlockSpec((1, 1, d), modrow(3)),
                  pl.BlockSpec((1, 1, d), modrow(4)),
                  pl.BlockSpec((1, d), const2),
                  pl.BlockSpec(wa16.shape, const2),
                  pl.BlockSpec(wb16.shape, const2),
                  pl.BlockSpec(wo16.shape, const2),
                  pl.BlockSpec(wqt16.shape, const2),
                  pl.BlockSpec(keys16.shape, lambda i: (0, 0, 0))],
        out_specs=[pl.BlockSpec((tm, d), lambda i: (i, 0)),
                   pl.BlockSpec((tm, d), lambda i: (i, 0)),
                   pl.BlockSpec((nhp, PEER_NKEYS, tm), lambda i: (0, 0, i))],
        out_shape=[jax.ShapeDtypeStruct((r, d), F32),
                   jax.ShapeDtypeStruct((r, d), BF16),
                   jax.ShapeDtypeStruct((nhp, PEER_NKEYS, r), F32)],
        name="merge",
        compiler_params=_params(("arbitrary",)),
    )(x2d, ypool, yhg, z, z, mod3, mod3, mod3, norm2_g, wa16, wb16, wo16, wqt16, keys16)


def _top16(v, iota_n):
    vals, idxs = [], []
    for _ in range(PEER_TOPK):
        m = jnp.max(v, axis=0, keepdims=True)
        idx = jnp.min(jnp.where(v == m, iota_n, float(PEER_NKEYS)), axis=0, keepdims=True)
        v = jnp.where(iota_n == idx, -jnp.inf, v)
        vals.append(m)
        idxs.append(idx)
    return jnp.concatenate(vals, axis=0), jnp.concatenate(idxs, axis=0)


def _topk_unit(s1, s2):
    tg = s1.shape[1]
    kk = PEER_TOPK
    nkf = float(PEER_NKEYS)
    iota_n = lax.broadcasted_iota(I32, (PEER_NKEYS, tg), 0).astype(F32)
    v1, n1 = _top16(s1, iota_n)
    v2, n2 = _top16(s2, iota_n)

    r = lax.broadcasted_iota(I32, (8, tg), 0)
    rf = r.astype(F32)
    sel3 = lambda x: jnp.where(r < 5, x[0], x[1])
    sel4 = lambda x: jnp.where(r < 4, x[0], jnp.where(r < 6, x[1], x[2]))
    sel5 = lambda x: jnp.where(r < 2, x[0], x[1])

    def tiles(l1, l2):
        row1 = lambda a: l1[a:a + 1]
        lo2, hi1 = l2[0:8], l1[8:16]
        hi1_r2 = pltpu.roll(hi1, 2, 0)
        t1 = [row1(0), row1(0), row1(1), sel3((row1(2), row1(4))), sel4((row1(3), row1(5), row1(6))),
              sel5((row1(7), hi1_r2)), hi1_r2]
        t2 = [lo2, l2[8:16], lo2, sel3((lo2, pltpu.roll(lo2, 5, 0))),
              sel4((lo2, pltpu.roll(lo2, 4, 0), pltpu.roll(lo2, 6, 0))), sel5((lo2, l2[0:1])), l2[0:1]]
        return t1, t2

    tv1, tv2 = tiles(v1, v2)
    tn1, tn2 = tiles(n1, n2)
    cand = [a + b for a, b in zip(tv1, tv2)]
    cand[6] = jnp.where(r < 2, cand[6], -jnp.inf)
    cand = jnp.concatenate(cand, axis=0)
    eid = jnp.concatenate([a * nkf + b for a, b in zip(tn1, tn2)], axis=0)
    pos = jnp.concatenate([rf, rf + 8.0, rf + 16.0, sel3((rf + 32.0, rf + 59.0)),
                           sel4((rf + 48.0, rf + 76.0, rf + 90.0)), sel5((rf + 112.0, rf * 16.0 + 96.0)),
                           rf * 16.0 + 224.0], axis=0)
    tops, es = [], []
    for _ in range(kk):
        m = jnp.max(cand, axis=0, keepdims=True)
        p = jnp.min(jnp.where(cand == m, pos, 1e9), axis=0, keepdims=True)
        sel = pos == p
        es.append(jnp.max(jnp.where(sel, eid, -1.0), axis=0, keepdims=True))
        cand = jnp.where(sel, -jnp.inf, cand)
        tops.append(m)
    top = jnp.concatenate(tops, axis=0)
    e = jnp.concatenate(es, axis=0)
    ex = jnp.exp(top - top[0:1])
    gate = ex / jnp.sum(ex, axis=0, keepdims=True)
    e1 = jnp.floor(e * (1.0 / PEER_NKEYS))
    return e1, e - e1 * nkf, gate * _SQRT_HALF


def _peer_kernel(s_ref, h2_ref, ut_ref, v_ref, x1_ref, g2_ref, fg_ref, o_ref,
                 w_scr, stage_scr, tr0_scr, tr1_scr, acc_scr, rt_scr, rr_scr):
    tb, d = h2_ref.shape
    eb = v_ref.shape[0]
    i = pl.program_id(0)
    e = pl.program_id(1)
    nk = PEER_NKEYS
    ngrp = tb // GROUP
    pack = w_scr.shape[2]
    unroll = stage_scr.shape[0]
    tr_scrs = (tr0_scr, tr1_scr)
    units = s_ref.shape[0] // 2
    steps_per_grp = PEER_HEADS // units

    @pl.when((i > 0) & (e == 0))
    def _build():
        rd = lax.rem(i + 1, 2)
        for fld in range(3):
            for g in range(ngrp):
                rr_scr[fld, g * GROUP:(g + 1) * GROUP, :] = rt_scr[rd, fld, g].T
        iota_s = lax.broadcasted_iota(I32, (nk, nk), 0).astype(F32).astype(BF16)
        zero = jnp.zeros((), BF16)
        one = jnp.ones((), BF16)

        def flush(tt):
            for g in range(unroll // pack):
                blk = stage_scr[g * pack:(g + 1) * pack].astype(BF16)
                tr_scrs[g][...] = jnp.transpose(blk, (1, 0, 2))
            for g in range(unroll // pack):
                w_scr[tt * (unroll // pack) + g] = tr_scrs[g][...]

        def body(tt, carry):
            flush(jnp.maximum(tt - 1, 0))
            ts = [tt * unroll + r for r in range(unroll)]
            row = lambda fld, t: jnp.broadcast_to(rr_scr[fld, pl.ds(t, 1), :], (pack, nk)).astype(BF16)
            tiles = [iota_s[v * pack:(v + 1) * pack] for v in range(nk // pack)]
            onehot = lambda idx, val: jnp.concatenate([jnp.where(tl == idx, val, zero) for tl in tiles], axis=0)
            g1t = [onehot(row(0, t), row(2, t)) for t in ts]
            p2t = [onehot(row(1, t), one) for t in ts]
            for r, (g, p) in enumerate(zip(g1t, p2t)):
                stage_scr[r] = _dot_nt(g, p)
            return carry

        stage_scr[...] = jnp.zeros(stage_scr.shape, F32)
        ntrip = tb // unroll
        lax.fori_loop(0, ntrip, body, 0)
        flush(ntrip - 1)
        acc_scr[...] = jnp.zeros(acc_scr.shape, F32)

    grp = e // steps_per_grp
    head0 = lax.rem(e, steps_per_grp) * units
    wr = lax.rem(i, 2)
    n1_0 = e * (eb // nk)
    nchunk = eb // (2 * nk)

    def route(k):
        i1, i2, gate = _topk_unit(s_ref[2 * k], s_ref[2 * k + 1])
        rows = pl.ds(pl.multiple_of((head0 + k) * PEER_TOPK, PEER_TOPK), PEER_TOPK)
        rt_scr[wr, 0, grp, rows, :] = i1
        rt_scr[wr, 1, grp, rows, :] = i2
        rt_scr[wr, 2, grp, rows, :] = gate

    def activate(k2, pieces):
        a2 = _dot(h2_ref[...], ut_ref[:, 2 * k2 * nk:2 * (k2 + 1) * nk])
        for k in (2 * k2, 2 * k2 + 1):
            ak = a2[:, (k - 2 * k2) * nk:(k - 2 * k2 + 1) * nk]
            wn = w_scr[:, n1_0 + k, :, :].reshape(tb, nk)
            act = ak * (1.0 + lax.erf(ak))
            pieces.append(wn * act.astype(BF16))

    @pl.when(i == 0)
    def _route_only():
        for k in range(units):
            route(k)

    @pl.when(i > 0)
    def _route_and_evaluate():
        pieces = []
        per = nchunk // units
        for k in range(units):
            route(k)
            for k2 in range(k * per, (k + 1) * per):
                activate(k2, pieces)
        wg = jnp.concatenate(pieces, axis=1)
        tn = 256
        for j in range(d // tn):
            acc_scr[:, j * tn:(j + 1) * tn] += _dot(wg, v_ref[:, j * tn:(j + 1) * tn])

    @pl.when((i > 0) & (e == pl.num_programs(1) - 1))
    def _fin():
        x2 = x1_ref[...] + g2_ref[0] * acc_scr[...]
        o_ref[...] = _rms(x2) * fg_ref[...]


def _peer_call(st, h2, ut16, v16, x1, mod3, final_g, seq, tb, eb):
    r, d = h2.shape
    ne = v16.shape[0]
    nblk = r // tb
    ngrp = tb // GROUP
    nsteps = ne // eb
    units = PEER_HEADS * ngrp // nsteps
    assert units * nsteps == PEER_HEADS * ngrp and PEER_HEADS % units == 0
    steps_per_grp = PEER_HEADS // units
    slots = PEER_HEADS * PEER_TOPK
    pack = 16
    prev = lambda i: jnp.maximum(i - 1, 0)
    tok = lambda i, e: (prev(i), 0)
    return pl.pallas_call(
        _peer_kernel,
        grid=(nblk + 1, nsteps),
        in_specs=[pl.BlockSpec((2 * units, PEER_NKEYS, GROUP),
                               lambda i, e: (lax.rem(e, steps_per_grp), 0,
                                             jnp.minimum(i, nblk - 1) * ngrp + e // steps_per_grp)),
                  pl.BlockSpec((tb, d), tok),
                  pl.BlockSpec((d, eb), lambda i, e: (0, e)),
                  pl.BlockSpec((eb, d), lambda i, e: (e, 0)),
                  pl.BlockSpec((tb, d), tok, pipeline_mode=pl.Buffered(1)),
                  pl.BlockSpec((1, 1, d), lambda i, e: (5 + N_MOD * ((prev(i) * tb) // seq), 0, 0)),
                  pl.BlockSpec((1, d), lambda i, e: (0, 0))],
        out_specs=pl.BlockSpec((tb, d), tok),
        out_shape=jax.ShapeDtypeStruct((r, d), F32),
        scratch_shapes=[pltpu.VMEM((tb // pack, PEER_NKEYS, pack, PEER_NKEYS), BF16),
                        pltpu.VMEM((2 * pack, PEER_NKEYS, PEER_NKEYS), F32),
                        pltpu.VMEM((PEER_NKEYS, pack, PEER_NKEYS), BF16),
                        pltpu.VMEM((PEER_NKEYS, pack, PEER_NKEYS), BF16),
                        pltpu.VMEM((tb, d), F32),
                        pltpu.VMEM((2, 3, ngrp, slots, GROUP), F32),
                        pltpu.VMEM((3, tb, slots), F32)],
        name="peer",
        compiler_params=_params(("arbitrary", "arbitrary")),
    )(st, h2, ut16, v16, x1, mod3, final_g)


def kernel(x, c, ctx, c_ctx, w_mod, b_mod, norm1_g, norm2_g, w_in, pool_w, pool_scale, hg_lb_logits, hg_norm_g,
           w_a, w_b, w_o, peer_wq, peer_keys, peer_u, peer_v, final_g):
    batch, seq, d = x.shape
    ctx_len = ctx.shape[1]
    assert w_mod.shape[0] == 1 and seq == GRID_W * GRID_W and hg_lb_logits.shape[1] == 2
    hw = HG_HEADS * GROUP

    cvec = jnp.concatenate([c, c_ctx[None, :], jnp.zeros((8 - batch - 1, d), F32)], axis=0)
    mod = _mod_call(cvec, w_mod[0], b_mod[0][None, :])
    mod3 = mod.reshape(8 * N_MOD, 1, d)

    w_in16 = w_in[0].astype(BF16)
    f0, f1 = 3 * hw, 5 * hw
    w_in16 = jnp.concatenate([w_in16[:, :f0], w_in16[:, f1:], w_in16[:, f0:f1]], axis=1)
    g1n = norm1_g[0][None, :]
    x2d = x.reshape(batch * seq, d)
    z16, zf = _inproj_call(x2d, mod3, 0, 1, seq, g1n, w_in16, 1024, 2 * hw)
    zc16, zcf = _inproj_call(ctx.reshape(batch * ctx_len, d), mod3, N_MOD * batch, N_MOD * batch + 1, None, g1n,
                             w_in16, 256, 2 * hw)

    lbl = hg_lb_logits.reshape(2 * hg_lb_logits.shape[1], hw)
    sf, sb = _hgrn_ctx_call(zc16, zcf, lbl, batch, ctx_len)
    yhg = _hgrn_call(z16, zf, lbl, hg_norm_g[0][None, :], sf, sb, batch, seq)
    ypool = _pool_call(z16, pool_w[0], pool_scale[0][None, :], batch, seq)

    nhp = 2 * PEER_HEADS
    x1, h2, st = _merge_call(x2d, ypool, yhg, z16, mod3, norm2_g[0][None, :],
                             w_a[0].astype(BF16), w_b[0].astype(BF16), w_o[0].astype(BF16),
                             peer_wq[0].T.astype(BF16),
                             peer_keys[0].reshape(nhp, PEER_NKEYS, GROUP).astype(BF16), seq, 512)
    out = _peer_call(st, h2, peer_u[0].T.astype(BF16), peer_v[0].astype(BF16), x1, mod3,
                     final_g[None, :], seq, 512, 2048)
    return out.reshape(batch, seq, d)
```

```python
import functools
import math

import jax
import jax.numpy as jnp
from jax import lax
from jax.experimental import pallas as pl
from jax.experimental.pallas import tpu as pltpu

F32 = jnp.float32
BF16 = jnp.bfloat16
I32 = jnp.int32

EPS = 1e-6
GRID_W = 64
POOL_WINDOWS = (2, 4, 8, 16)
POOL_PAD = 8
GROUP = 128
HG_HEADS = 4
HG_CHUNK = 16
HG_BLOCK = 128
HG_MID = 8
PEER_HEADS = 8
PEER_NKEYS = 128
PEER_TOPK = 16
N_MOD = 6
VMEM_LIMIT = 60 * 1024 * 1024
TM_INPROJ = 2048
TM_CTX = 256
TM_MERGE = 512
TB_PEER = 512
EB_PEER = 2048

_NT = (((1,), (1,)), ((), ()))
_SQRT_HALF = math.sqrt(0.5)


def _dot(a, b):
    return jnp.dot(a, b, preferred_element_type=F32)


def _dot_nt(a, b):
    return lax.dot_general(a, b, _NT, preferred_element_type=F32)


def _hi_lo(a):
    hi = a.astype(BF16)
    return hi, (a - hi.astype(F32)).astype(BF16)


def _sigmoid(x):
    return 1.0 / (1.0 + jnp.exp(-x))


def _rms(x):
    return x * lax.rsqrt(jnp.mean(x * x, axis=-1, keepdims=True) + EPS)


def _params(sem):
    return pltpu.CompilerParams(dimension_semantics=sem, vmem_limit_bytes=VMEM_LIMIT)


def _mod_kernel(c_ref, w_ref, b_ref, o_ref):
    c = c_ref[...]
    s = c * _sigmoid(c)
    s_hi, s_lo = _hi_lo(s)
    w_hi, w_lo = _hi_lo(w_ref[...])
    o_ref[...] = _dot(s_hi, w_hi) + (_dot(s_hi, w_lo) + _dot(s_lo, w_hi)) + b_ref[...]


def _mod_call(cvec, w_mod, b_mod):
    rows, d = cvec.shape
    n = w_mod.shape[1]
    tn = n // 4
    return pl.pallas_call(
        _mod_kernel,
        grid=(n // tn,),
        in_specs=[pl.BlockSpec((rows, d), lambda j: (0, 0)),
                  pl.BlockSpec((d, tn), lambda j: (0, j)),
                  pl.BlockSpec((1, tn), lambda j: (0, j))],
        out_specs=pl.BlockSpec((rows, tn), lambda j: (0, j)),
        out_shape=jax.ShapeDtypeStruct((rows, n), F32),
        name="mod",
        compiler_params=_params(("arbitrary",)),
    )(cvec, w_mod, b_mod)


def _inproj_kernel(x_ref, sh_ref, sc_ref, g_ref, w_ref, o16_ref, of_ref, h_scr):
    j = pl.program_id(1)
    last = pl.num_programs(1) - 1

    @pl.when(j == 0)
    def _():
        h = _rms(x_ref[...]) * g_ref[...] * (1.0 + sc_ref[0]) + sh_ref[0]
        h_scr[...] = h.astype(BF16)

    y = _dot(h_scr[...], w_ref[...])

    @pl.when(j < last)
    def _():
        o16_ref[...] = y.astype(BF16)

    @pl.when(j == last)
    def _():
        of_ref[...] = y


def _inproj_call(x2d, mod3, sh_row, sc_row, rows_per_mod, g, w16, tm, tn):
    r, d = x2d.shape
    ncols = w16.shape[1] // tn
    if rows_per_mod is None:
        sh_map = lambda i, j: (sh_row, 0, 0)
        sc_map = lambda i, j: (sc_row, 0, 0)
    else:
        sh_map = lambda i, j: (sh_row + N_MOD * ((i * tm) // rows_per_mod), 0, 0)
        sc_map = lambda i, j: (sc_row + N_MOD * ((i * tm) // rows_per_mod), 0, 0)
    return pl.pallas_call(
        _inproj_kernel,
        grid=(r // tm, ncols),
        in_specs=[pl.BlockSpec((tm, d), lambda i, j: (i, 0)),
                  pl.BlockSpec((1, 1, d), sh_map),
                  pl.BlockSpec((1, 1, d), sc_map),
                  pl.BlockSpec((1, d), lambda i, j: (0, 0)),
                  pl.BlockSpec((d, tn), lambda i, j: (0, j))],
        out_specs=[pl.BlockSpec((tm, tn), lambda i, j: (i, jnp.minimum(j, ncols - 2))),
                   pl.BlockSpec((tm, tn), lambda i, j: (i, 0))],
        out_shape=[jax.ShapeDtypeStruct((r, (ncols - 1) * tn), BF16),
                   jax.ShapeDtypeStruct((r, tn), F32)],
        scratch_shapes=[pltpu.VMEM((tm, d), BF16)],
        name="inproj",
        compiler_params=_params(("arbitrary", "arbitrary")),
    )(x2d, mod3, mod3, g, w16)


def _pool_kernel(p_ref, w_ref, sc_ref, o_ref, pad_scr, tmp_scr):
    gw, pp = GRID_W, POOL_PAD
    pad_scr[...] = jnp.zeros(pad_scr.shape, F32)
    r_i = lax.broadcasted_iota(I32, (gw, gw, GROUP), 0)
    c_i = lax.broadcasted_iota(I32, (gw, gw, GROUP), 1)
    for g, win in enumerate(POOL_WINDOWS):
        half = win // 2
        pg = p_ref[:, g * GROUP:(g + 1) * GROUP].astype(F32).reshape(gw, gw, GROUP)
        pad_scr[pp:pp + gw, pp:pp + gw, :] = pg
        acc = None
        for k in range(-half, win - half):
            sl = pad_scr[pp + k:pp + k + gw, :, :]
            acc = sl if acc is None else acc + sl
        tmp_scr[...] = acc
        acc = None
        for k in range(-half, win - half):
            sl = tmp_scr[:, pp + k:pp + k + gw, :]
            acc = sl if acc is None else acc + sl
        cr = jnp.minimum(r_i - half + win, gw) - jnp.maximum(r_i - half, 0)
        cc = jnp.minimum(c_i - half + win, gw) - jnp.maximum(c_i - half, 0)
        mean = acc / (cr * cc).astype(F32)
        d = (mean - pg).reshape(gw * gw, GROUP)
        y = _dot(d.astype(BF16), w_ref[g].astype(BF16)) * sc_ref[:, g * GROUP:(g + 1) * GROUP]
        o_ref[:, g * GROUP:(g + 1) * GROUP] = y.astype(BF16)


def _pool_call(z, pool_w, pool_scale, batch, seq):
    width = len(POOL_WINDOWS) * GROUP
    ext = GRID_W + 2 * POOL_PAD
    return pl.pallas_call(
        _pool_kernel,
        grid=(batch,),
        in_specs=[pl.BlockSpec((seq, width), lambda b: (b, 0)),
                  pl.BlockSpec(pool_w.shape, lambda b: (0, 0, 0)),
                  pl.BlockSpec((1, width), lambda b: (0, 0))],
        out_specs=pl.BlockSpec((seq, width), lambda b: (b, 0)),
        out_shape=jax.ShapeDtypeStruct((batch * seq, width), BF16),
        scratch_shapes=[pltpu.VMEM((ext, ext, GROUP), F32), pltpu.VMEM((GRID_W, ext, GROUP), F32)],
        name="pool",
        compiler_params=_params(("arbitrary",)),
    )(z, pool_w, pool_scale)


def _hgrn_masks(rev):
    t, c = HG_BLOCK, HG_CHUNK
    row = lax.broadcasted_iota(I32, (t, t), 0)
    col = lax.broadcasted_iota(I32, (t, t), 1)
    sh = c.bit_length() - 1
    same = (row >> sh) == (col >> sh)
    tri = (col >= row) if rev else (col <= row)
    mid = ((row >> sh) << sh) + HG_MID
    to_mid = (col >= mid) if rev else (col <= mid)
    m_all = jnp.concatenate([jnp.where(same & tri, 1.0, 0.0),
                             jnp.where(same, 1.0, 0.0),
                             jnp.where(same & to_mid, 1.0, 0.0)], axis=0).astype(BF16)
    return m_all, same & tri, col >> sh


def _hgrn_blocks(chains, masks, emit_out):
    t, c = HG_BLOCK, HG_CHUNK
    nch = t // c
    nc = len(chains)
    revs = [ch[0] for ch in chains]
    qs = [ch[2] for ch in chains]
    vs = [ch[3] for ch in chains]
    sts = [ch[5] for ch in chains]
    ks, bs = [], []
    for rev, x_f, _, _, lb, _ in chains:
        f = lb + (1.0 - lb) * _sigmoid(x_f)
        g_hi, g_lo = _hi_lo(jnp.log(f))
        m_all = masks[rev][0]
        ks.append(1.0 - f)
        bs.append(_dot(m_all, g_hi) + _dot(m_all, g_lo))
    b_loc = [b[0:t] for b in bs]
    b_tot = [b[t:2 * t] for b in bs]
    b_mid = [b[2 * t:3 * t] for b in bs]
    kd16 = [(ks[i] * jnp.exp(b_tot[i] - b_loc[i])).astype(BF16) for i in range(nc)]
    vt = [v.T for v in vs]
    a_all = [jnp.exp(b) for b in b_tot]
    os_ = [None] * nc
    if emit_out:
        a16 = [(qs[i] * jnp.exp(b_loc[i])).astype(BF16) for i in range(nc)]
        qd16 = [(qs[i] * jnp.exp(b_loc[i] - b_mid[i])).astype(BF16) for i in range(nc)]
        kg16 = [(ks[i] * jnp.exp(b_mid[i] - b_loc[i])).astype(BF16) for i in range(nc)]
        attn = [jnp.where(masks[revs[i]][1], _dot_nt(qd16[i], kg16[i]), 0.0) for i in range(nc)]
        os_ = [_dot(attn[i].astype(BF16), vs[i].astype(BF16)) for i in range(nc)]
        o_inter = [[None] * nch for _ in range(nc)]
    for step in range(nch):
        for i in range(nc):
            n = nch - 1 - step if revs[i] else step
            if emit_out:
                o_inter[i][n] = _dot_nt(a16[i][n * c:(n + 1) * c], sts[i].astype(BF16))
            ut = _dot(jnp.where(masks[revs[i]][2] == n, vt[i], 0.0).astype(BF16), kd16[i])
            sts[i] = sts[i] * a_all[i][n * c:n * c + 1, :] + ut
    if emit_out:
        os_ = [os_[i] + jnp.concatenate(o_inter[i], axis=0) for i in range(nc)]
    return list(zip(os_, sts))


def _lower_bounds(lbl_ref):
    out = []
    for d in range(2):
        l0, l1 = lbl_ref[2 * d:2 * d + 1, :], lbl_ref[2 * d + 1:2 * d + 2, :]
        m = jnp.maximum(l0, l1)
        e0, e1 = jnp.exp(l0 - m), jnp.exp(l1 - m)
        out.append(e0 / (e0 + e1))
    return out


def _hgrn_run(nheads, seq, lbl_ref, f_refs, q_ref, v_ref, st_scr, o_scr):
    t = HG_BLOCK
    nblk = seq // t
    masks = (_hgrn_masks(False), _hgrn_masks(True))
    lbs = _lower_bounds(lbl_ref)
    emit_out = o_scr is not None

    def body(i, carry, accumulate):
        chains = []
        for d in range(2):
            blk = (nblk - 1 - i) if d else i
            rows = pl.ds(pl.multiple_of(blk * t, t), t)
            for h in range(nheads):
                ls = slice(h * GROUP, (h + 1) * GROUP)
                q = q_ref[rows, ls].astype(F32) if emit_out else None
                chains.append((d, h, rows, ls, f_refs[d][rows, ls], q, v_ref[rows, ls].astype(F32),
                               st_scr[d * nheads + h]))
        results = _hgrn_blocks([(d, x_f, q, v, lbs[d][:, ls], st) for d, h, rows, ls, x_f, q, v, st in chains],
                               masks, emit_out)
        for (d, h, rows, ls, *_), (o, st) in zip(chains, results):
            st_scr[d * nheads + h] = st
            if emit_out and accumulate:
                o_scr[rows, ls] += o
            elif emit_out:
                o_scr[rows, ls] = o
        return carry

    if emit_out:
        assert nblk % 2 == 0
        lax.fori_loop(0, nblk // 2, functools.partial(body, accumulate=False), 0)
        lax.fori_loop(nblk // 2, nblk, functools.partial(body, accumulate=True), 0)
    else:
        lax.fori_loop(0, nblk, functools.partial(body, accumulate=False), 0)


def _hgrn_ctx_kernel(v_ref, ff_ref, fb_ref, lbl_ref, sf_ref, sb_ref, st_scr, *, seq):
    nheads = sf_ref.shape[1]
    st_scr[...] = jnp.zeros(st_scr.shape, F32)
    _hgrn_run(nheads, seq, lbl_ref, (ff_ref, fb_ref), None, v_ref, st_scr, None)
    sf_ref[0] = st_scr[0:nheads]
    sb_ref[0] = st_scr[nheads:2 * nheads]


def _hgrn_kernel(q_ref, v_ref, ff_ref, fb_ref, og_ref, lbl_ref, ng_ref, sf_ref, sb_ref, y_ref,
                 st_scr, o_scr, *, seq):
    nheads = sf_ref.shape[1]
    st_scr[0:nheads] = sf_ref[0]
    st_scr[nheads:2 * nheads] = sb_ref[0]
    _hgrn_run(nheads, seq, lbl_ref, (ff_ref, fb_ref), q_ref, v_ref, st_scr, o_scr)
    rb = 4 * HG_BLOCK

    def epilogue(i, carry):
        rows = pl.ds(pl.multiple_of(i * rb, rb), rb)
        for h in range(nheads):
            ls = slice(h * GROUP, (h + 1) * GROUP)
            og = og_ref[rows, ls].astype(F32)
            y = _rms(o_scr[rows, ls]) * ng_ref[...] * (og * _sigmoid(og))
            y_ref[rows, ls] = y.astype(BF16)
        return carry

    lax.fori_loop(0, seq // rb, epilogue, 0)


def _hgrn_specs(seq):
    hw = HG_HEADS * GROUP
    big = lambda k: pl.BlockSpec((seq, hw), lambda b: (b, k), pipeline_mode=pl.Buffered(1))
    st_spec = pl.BlockSpec((1, HG_HEADS, GROUP, GROUP), lambda b: (b, 0, 0, 0))
    return hw, big, st_spec


def _hgrn_ctx_call(zc16, zcf, lbl, batch, seq):
    hw, big, st_spec = _hgrn_specs(seq)
    st_shape = jax.ShapeDtypeStruct((batch, HG_HEADS, GROUP, GROUP), F32)
    return pl.pallas_call(
        functools.partial(_hgrn_ctx_kernel, seq=seq),
        grid=(batch,),
        in_specs=[big(0), big(0), big(1), pl.BlockSpec((4, hw), lambda b: (0, 0))],
        out_specs=[st_spec, st_spec],
        out_shape=[st_shape, st_shape],
        scratch_shapes=[pltpu.VMEM((2 * HG_HEADS, GROUP, GROUP), F32)],
        name="hgrn_ctx",
        compiler_params=_params(("arbitrary",)),
    )(zc16, zcf, zcf, lbl)


def _hgrn_call(z16, zf, lbl, norm_g, sf, sb, batch, seq):
    hw, big, st_spec = _hgrn_specs(seq)
    return pl.pallas_call(
        functools.partial(_hgrn_kernel, seq=seq),
        grid=(batch,),
        in_specs=[big(1), big(2), big(0), big(1), big(3),
                  pl.BlockSpec((4, hw), lambda b: (0, 0)),
                  pl.BlockSpec((1, GROUP), lambda b: (0, 0)),
                  st_spec, st_spec],
        out_specs=pl.BlockSpec((seq, hw), lambda b: (b, 0)),
        out_shape=jax.ShapeDtypeStruct((batch * seq, hw), BF16),
        scratch_shapes=[pltpu.VMEM((2 * HG_HEADS, GROUP, GROUP), F32), pltpu.VMEM((seq, hw), F32)],
        name="hgrn",
        compiler_params=_params(("arbitrary",)),
    )(z16, z16, zf, zf, z16, lbl, norm_g, sf, sb)


def _merge_kernel(x_ref, yp_ref, yh_ref, gp_ref, gh_ref, g1_ref, sh_ref, sc_ref, ng_ref,
                  wa_ref, wb_ref, wo_ref, wqt_ref, keys_ref, x1_ref, h2_ref, st_ref):
    m = (_sigmoid(gp_ref[...].astype(F32)) * _dot(yp_ref[...], wa_ref[...])
         + _sigmoid(gh_ref[...].astype(F32)) * _dot(yh_ref[...], wb_ref[...]))
    x1 = x_ref[...] + g1_ref[0] * _dot(m.astype(BF16), wo_ref[...])
    x1_ref[...] = x1
    h2 = _rms(x1) * ng_ref[...] * (1.0 + sc_ref[0]) + sh_ref[0]
    h2_ref[...] = (h2 * _SQRT_HALF).astype(BF16)
    qt = _dot(wqt_ref[...], h2.T.astype(BF16))
    for hp in range(2 * PEER_HEADS):
        st_ref[hp] = _dot(keys_ref[hp], qt[hp * GROUP:(hp + 1) * GROUP].astype(BF16))


def _merge_call(x2d, ypool, yhg, z, mod3, norm2_g, wa16, wb16, wo16, wqt16, keys16, seq, tm):
    r, d = x2d.shape
    gcol0 = (z.shape[1] - 2 * d) // d
    modrow = lambda k: (lambda i: (k + N_MOD * ((i * tm) // seq), 0, 0))
    const2 = lambda i: (0, 0)
    nhp = 2 * PEER_HEADS
    return pl.pallas_call(
        _merge_kernel,
        grid=(r // tm,),
        in_specs=[pl.BlockSpec((tm, d), lambda i: (i, 0)),
                  pl.BlockSpec((tm, ypool.shape[1]), lambda i: (i, 0)),
                  pl.BlockSpec((tm, yhg.shape[1]), lambda i: (i, 0)),
                  pl.BlockSpec((tm, d), lambda i: (i, gcol0)),
                  pl.BlockSpec((tm, d), lambda i: (i, gcol0 + 1)),
                  pl.BlockSpec((1, 1, d), modrow(2)),
                  pl.BlockSpec((1, 1, d), modrow(3)),
                  pl.BlockSpec((1, 1, d), modrow(4)),
                  pl.BlockSpec((1, d), const2),
                  pl.BlockSpec(wa16.shape, const2),
                  pl.BlockSpec(wb16.shape, const2),
                  pl.BlockSpec(wo16.shape, const2),
                  pl.BlockSpec(wqt16.shape, const2),
                  pl.BlockSpec(keys16.shape, lambda i: (0, 0, 0))],
        out_specs=[pl.BlockSpec((tm, d), lambda i: (i, 0)),
                   pl.BlockSpec((tm, d), lambda i: (i, 0)),
                   pl.BlockSpec((nhp, PEER_NKEYS, tm), lambda i: (0, 0, i))],
        out_shape=[jax.ShapeDtypeStruct((r, d), F32),
                   jax.ShapeDtypeStruct((r, d), BF16),
                   jax.ShapeDtypeStruct((nhp, PEER_NKEYS, r), F32)],
        name="merge",
        compiler_params=_params(("arbitrary",)),
    )(x2d, ypool, yhg, z, z, mod3, mod3, mod3, norm2_g, wa16, wb16, wo16, wqt16, keys16)


def _top16(v, iota_n):
    vals, idxs = [], []
    for _ in range(PEER_TOPK):
        m = jnp.max(v, axis=0, keepdims=True)
        idx = jnp.min(jnp.where(v == m, iota_n, float(PEER_NKEYS)), axis=0, keepdims=True)
        v = jnp.where(iota_n == idx, -jnp.inf, v)
        vals.append(m)
        idxs.append(idx)
    return jnp.concatenate(vals, axis=0), jnp.concatenate(idxs, axis=0)


def _topk_unit(s1, s2):
    tg = s1.shape[1]
    kk = PEER_TOPK
    nkf = float(PEER_NKEYS)
    iota_n = lax.broadcasted_iota(I32, (PEER_NKEYS, tg), 0).astype(F32)
    v1, n1 = _top16(s1, iota_n)
    v2, n2 = _top16(s2, iota_n)

    r = lax.broadcasted_iota(I32, (8, tg), 0)
    rf = r.astype(F32)
    sel3 = lambda x: jnp.where(r < 5, x[0], x[1])
    sel4 = lambda x: jnp.where(r < 4, x[0], jnp.where(r < 6, x[1], x[2]))
    sel5 = lambda x: jnp.where(r < 2, x[0], x[1])

    def tiles(l1, l2):
        row1 = lambda a: l1[a:a + 1]
        lo2, hi1 = l2[0:8], l1[8:16]
        hi1_r2 = pltpu.roll(hi1, 2, 0)
        t1 = [row1(0), row1(0), row1(1), sel3((row1(2), row1(4))), sel4((row1(3), row1(5), row1(6))),
              sel5((row1(7), hi1_r2)), hi1_r2]
        t2 = [lo2, l2[8:16], lo2, sel3((lo2, pltpu.roll(lo2, 5, 0))),
              sel4((lo2, pltpu.roll(lo2, 4, 0), pltpu.roll(lo2, 6, 0))), sel5((lo2, l2[0:1])), l2[0:1]]
        return t1, t2

    tv1, tv2 = tiles(v1, v2)
    tn1, tn2 = tiles(n1, n2)
    cand = [a + b for a, b in zip(tv1, tv2)]
    cand[6] = jnp.where(r < 2, cand[6], -jnp.inf)
    cand = jnp.concatenate(cand, axis=0)
    eid = jnp.concatenate([a * nkf + b for a, b in zip(tn1, tn2)], axis=0)
    pos = jnp.concatenate([rf, rf + 8.0, rf + 16.0, sel3((rf + 32.0, rf + 59.0)),
                           sel4((rf + 48.0, rf + 76.0, rf + 90.0)), sel5((rf + 112.0, rf * 16.0 + 96.0)),
                           rf * 16.0 + 224.0], axis=0)
    tops, es = [], []
    for _ in range(kk):
        m = jnp.max(cand, axis=0, keepdims=True)
        p = jnp.min(jnp.where(cand == m, pos, 1e9), axis=0, keepdims=True)
        sel = pos == p
        es.append(jnp.max(jnp.where(sel, eid, -1.0), axis=0, keepdims=True))
        cand = jnp.where(sel, -jnp.inf, cand)
        tops.append(m)
    top = jnp.concatenate(tops, axis=0)
    e = jnp.concatenate(es, axis=0)
    ex = jnp.exp(top - top[0:1])
    gate = ex / jnp.sum(ex, axis=0, keepdims=True)
    e1 = jnp.floor(e * (1.0 / PEER_NKEYS))
    return e1, e - e1 * nkf, gate * _SQRT_HALF


def _peer_kernel(s_ref, h2_ref, ut_ref, v_ref, x1_ref, g2_ref, fg_ref, o_ref,
                 w_scr, stage_scr, tr0_scr, tr1_scr, acc_scr, rt_scr, rr_scr):
    tb, d = h2_ref.shape
    eb = v_ref.shape[0]
    i = pl.program_id(0)
    e = pl.program_id(1)
    nk = PEER_NKEYS
    ngrp = tb // GROUP
    pack = w_scr.shape[2]
    unroll = stage_scr.shape[0]
    tr_scrs = (tr0_scr, tr1_scr)
    units = s_ref.shape[0] // 2
    steps_per_grp = PEER_HEADS // units

    @pl.when((i > 0) & (e == 0))
    def _build():
        rd = lax.rem(i + 1, 2)
        for fld in range(3):
            for g in range(ngrp):
                rr_scr[fld, g * GROUP:(g + 1) * GROUP, :] = rt_scr[rd, fld, g].T
        iota_s = lax.broadcasted_iota(I32, (nk, nk), 0).astype(F32).astype(BF16)
        zero = jnp.zeros((), BF16)
        one = jnp.ones((), BF16)

        def flush(tt):
            for g in range(unroll // pack):
                blk = stage_scr[g * pack:(g + 1) * pack].astype(BF16)
                tr_scrs[g][...] = jnp.transpose(blk, (1, 0, 2))
            for g in range(unroll // pack):
                w_scr[tt * (unroll // pack) + g] = tr_scrs[g][...]

        def body(tt, carry):
            flush(jnp.maximum(tt - 1, 0))
            ts = [tt * unroll + r for r in range(unroll)]
            row = lambda fld, t: jnp.broadcast_to(rr_scr[fld, pl.ds(t, 1), :], (pack, nk)).astype(BF16)
            tiles = [iota_s[v * pack:(v + 1) * pack] for v in range(nk // pack)]
            onehot = lambda idx, val: jnp.concatenate([jnp.where(tl == idx, val, zero) for tl in tiles], axis=0)
            g1t = [onehot(row(0, t), row(2, t)) for t in ts]
            p2t = [onehot(row(1, t), one) for t in ts]
            for r, (g, p) in enumerate(zip(g1t, p2t)):
                stage_scr[r] = _dot_nt(g, p)
            return carry

        stage_scr[...] = jnp.zeros(stage_scr.shape, F32)
        ntrip = tb // unroll
        lax.fori_loop(0, ntrip, body, 0)
        flush(ntrip - 1)
        acc_scr[...] = jnp.zeros(acc_scr.shape, F32)

    grp = e // steps_per_grp
    head0 = lax.rem(e, steps_per_grp) * units
    wr = lax.rem(i, 2)
    n1_0 = e * (eb // nk)
    nchunk = eb // (2 * nk)

    def route(k):
        i1, i2, gate = _topk_unit(s_ref[2 * k], s_ref[2 * k + 1])
        rows = pl.ds(pl.multiple_of((head0 + k) * PEER_TOPK, PEER_TOPK), PEER_TOPK)
        rt_scr[wr, 0, grp, rows, :] = i1
        rt_scr[wr, 1, grp, rows, :] = i2
        rt_scr[wr, 2, grp, rows, :] = gate

    def activate(k2, pieces):
        a2 = _dot(h2_ref[...], ut_ref[:, 2 * k2 * nk:2 * (k2 + 1) * nk])
        for k in (2 * k2, 2 * k2 + 1):
            ak = a2[:, (k - 2 * k2) * nk:(k - 2 * k2 + 1) * nk]
            wn = w_scr[:, n1_0 + k, :, :].reshape(tb, nk)
            act = ak * (1.0 + lax.erf(ak))
            pieces.append(wn * act.astype(BF16))

    @pl.when(i == 0)
    def _route_only():
        for k in range(units):
            route(k)

    @pl.when(i > 0)
    def _route_and_evaluate():
        pieces = []
        per = nchunk // units
        for k in range(units):
            route(k)
            for k2 in range(k * per, (k + 1) * per):
                activate(k2, pieces)
        wg = jnp.concatenate(pieces, axis=1)
        tn = 256
        for j in range(d // tn):
            acc_scr[:, j * tn:(j + 1) * tn] += _dot(wg, v_ref[:, j * tn:(j + 1) * tn])

    @pl.when((i > 0) & (e == pl.num_programs(1) - 1))
    def _fin():
        x2 = x1_ref[...] + g2_ref[0] * acc_scr[...]
        o_ref[...] = _rms(x2) * fg_ref[...]


def _peer_call(st, h2, ut16, v16, x1, mod3, final_g, seq, tb, eb):
    r, d = h2.shape
    ne = v16.shape[0]
    nblk = r // tb
    ngrp = tb // GROUP
    nsteps = ne // eb
    units = PEER_HEADS * ngrp // nsteps
    assert units * nsteps == PEER_HEADS * ngrp and PEER_HEADS % units == 0
    steps_per_grp = PEER_HEADS // units
    slots = PEER_HEADS * PEER_TOPK
    pack = 16
    prev = lambda i: jnp.maximum(i - 1, 0)
    tok = lambda i, e: (prev(i), 0)
    return pl.pallas_call(
        _peer_kernel,
        grid=(nblk + 1, nsteps),
        in_specs=[pl.BlockSpec((2 * units, PEER_NKEYS, GROUP),
                               lambda i, e: (lax.rem(e, steps_per_grp), 0,
                                             jnp.minimum(i, nblk - 1) * ngrp + e // steps_per_grp)),
                  pl.BlockSpec((tb, d), tok),
                  pl.BlockSpec((d, eb), lambda i, e: (0, e)),
                  pl.BlockSpec((eb, d), lambda i, e: (e, 0)),
                  pl.BlockSpec((tb, d), tok, pipeline_mode=pl.Buffered(1)),
                  pl.BlockSpec((1, 1, d), lambda i, e: (5 + N_MOD * ((prev(i) * tb) // seq), 0, 0)),
                  pl.BlockSpec((1, d), lambda i, e: (0, 0))],
        out_specs=pl.BlockSpec((tb, d), tok),
        out_shape=jax.ShapeDtypeStruct((r, d), F32),
        scratch_shapes=[pltpu.VMEM((tb // pack, PEER_NKEYS, pack, PEER_NKEYS), BF16),
                        pltpu.VMEM((2 * pack, PEER_NKEYS, PEER_NKEYS), F32),
                        pltpu.VMEM((PEER_NKEYS, pack, PEER_NKEYS), BF16),
                        pltpu.VMEM((PEER_NKEYS, pack, PEER_NKEYS), BF16),
                        pltpu.VMEM((tb, d), F32),
                        pltpu.VMEM((2, 3, ngrp, slots, GROUP), F32),
                        pltpu.VMEM((3, tb, slots), F32)],
        name="peer",
        compiler_params=_params(("arbitrary", "arbitrary")),
    )(st, h2, ut16, v16, x1, mod3, final_g)


def kernel(x, c, ctx, c_ctx, w_mod, b_mod, norm1_g, norm2_g, w_in, pool_w, pool_scale, hg_lb_logits, hg_norm_g,
           w_a, w_b, w_o, peer_wq, peer_keys, peer_u, peer_v, final_g):
    batch, seq, d = x.shape
    ctx_len = ctx.shape[1]
    assert w_mod.shape[0] == 1 and seq == GRID_W * GRID_W and hg_lb_logits.shape[1] == 2
    hw = HG_HEADS * GROUP

    cvec = jnp.concatenate([c, c_ctx[None, :], jnp.zeros((8 - batch - 1, d), F32)], axis=0)
    mod = _mod_call(cvec, w_mod[0], b_mod[0][None, :])
    mod3 = mod.reshape(8 * N_MOD, 1, d)

    w_in16 = w_in[0].astype(BF16)
    f0, f1 = 3 * hw, 5 * hw
    w_in16 = jnp.concatenate([w_in16[:, :f0], w_in16[:, f1:], w_in16[:, f0:f1]], axis=1)
    g1n = norm1_g[0][None, :]
    x2d = x.reshape(batch * seq, d)
    z16, zf = _inproj_call(x2d, mod3, 0, 1, seq, g1n, w_in16, TM_INPROJ, 2 * hw)
    w_ctx16 = jnp.concatenate([w_in16[:, 2 * hw:4 * hw], w_in16[:, -2 * hw:]], axis=1)
    zc16, zcf = _inproj_call(ctx.reshape(batch * ctx_len, d), mod3, N_MOD * batch, N_MOD * batch + 1, None, g1n,
                             w_ctx16, TM_CTX, 2 * hw)

    lbl = hg_lb_logits.reshape(2 * hg_lb_logits.shape[1], hw)
    sf, sb = _hgrn_ctx_call(zc16, zcf, lbl, batch, ctx_len)
    yhg = _hgrn_call(z16, zf, lbl, hg_norm_g[0][None, :], sf, sb, batch, seq)
    ypool = _pool_call(z16, pool_w[0], pool_scale[0][None, :], batch, seq)

    nhp = 2 * PEER_HEADS
    x1, h2, st = _merge_call(x2d, ypool, yhg, z16, mod3, norm2_g[0][None, :],
                             w_a[0].astype(BF16), w_b[0].astype(BF16), w_o[0].astype(BF16),
                             peer_wq[0].T.astype(BF16),
                             peer_keys[0].reshape(nhp, PEER_NKEYS, GROUP).astype(BF16), seq, TM_MERGE)
    out = _peer_call(st, h2, peer_u[0].T.astype(BF16), peer_v[0].astype(BF16), x1, mod3,
                     final_g[None, :], seq, TB_PEER, EB_PEER)
    return out.reshape(batch, seq, d)
```

```python
import functools
import math

import jax
import jax.numpy as jnp
from jax import lax
from jax.experimental import pallas as pl
from jax.experimental.pallas import tpu as pltpu

F32 = jnp.float32
BF16 = jnp.bfloat16
I32 = jnp.int32

EPS = 1e-6
GRID_W = 64
POOL_WINDOWS = (2, 4, 8, 16)
POOL_PAD = 8
GROUP = 128
HG_HEADS = 4
HG_CHUNK = 16
HG_BLOCK = 128
HG_MID = 8
PEER_HEADS = 8
PEER_NKEYS = 128
PEER_TOPK = 16
N_MOD = 6
VMEM_LIMIT = 60 * 1024 * 1024
TM_INPROJ = 2048
TM_CTX = 256
TM_MERGE = 512
TB_PEER = 512
EB_PEER = 2048

_NT = (((1,), (1,)), ((), ()))
_SQRT_HALF = math.sqrt(0.5)


def _dot(a, b):
    return jnp.dot(a, b, preferred_element_type=F32)


def _dot_nt(a, b):
    return lax.dot_general(a, b, _NT, preferred_element_type=F32)


def _hi_lo(a):
    hi = a.astype(BF16)
    return hi, (a - hi.astype(F32)).astype(BF16)


def _sigmoid(x):
    return 1.0 / (1.0 + jnp.exp(-x))


def _rms(x):
    return x * lax.rsqrt(jnp.mean(x * x, axis=-1, keepdims=True) + EPS)


def _params(sem):
    return pltpu.CompilerParams(dimension_semantics=sem, vmem_limit_bytes=VMEM_LIMIT)


def _mod_kernel(c_ref, w_ref, b_ref, o_ref):
    c = c_ref[...]
    s = c * _sigmoid(c)
    s_hi, s_lo = _hi_lo(s)
    w_hi, w_lo = _hi_lo(w_ref[...])
    o_ref[...] = _dot(s_hi, w_hi) + (_dot(s_hi, w_lo) + _dot(s_lo, w_hi)) + b_ref[...]


def _mod_call(cvec, w_mod, b_mod):
    rows, d = cvec.shape
    n = w_mod.shape[1]
    tn = n // 4
    return pl.pallas_call(
        _mod_kernel,
        grid=(n // tn,),
        in_specs=[pl.BlockSpec((rows, d), lambda j: (0, 0)),
                  pl.BlockSpec((d, tn), lambda j: (0, j)),
                  pl.BlockSpec((1, tn), lambda j: (0, j))],
        out_specs=pl.BlockSpec((rows, tn), lambda j: (0, j)),
        out_shape=jax.ShapeDtypeStruct((rows, n), F32),
        name="mod",
        compiler_params=_params(("arbitrary",)),
    )(cvec, w_mod, b_mod)


def _inproj_kernel(x_ref, sh_ref, sc_ref, g_ref, w_ref, o16_ref, of_ref, h_scr):
    j = pl.program_id(1)
    last = pl.num_programs(1) - 1

    @pl.when(j == 0)
    def _():
        h = _rms(x_ref[...]) * g_ref[...] * (1.0 + sc_ref[0]) + sh_ref[0]
        h_scr[...] = h.astype(BF16)

    y = _dot(h_scr[...], w_ref[...])

    @pl.when(j < last)
    def _():
        o16_ref[...] = y.astype(BF16)

    @pl.when(j == last)
    def _():
        of_ref[...] = y


def _inproj_call(x2d, mod3, sh_row, sc_row, rows_per_mod, g, w16, tm, tn):
    r, d = x2d.shape
    ncols = w16.shape[1] // tn
    if rows_per_mod is None:
        sh_map = lambda i, j: (sh_row, 0, 0)
        sc_map = lambda i, j: (sc_row, 0, 0)
    else:
        sh_map = lambda i, j: (sh_row + N_MOD * ((i * tm) // rows_per_mod), 0, 0)
        sc_map = lambda i, j: (sc_row + N_MOD * ((i * tm) // rows_per_mod), 0, 0)
    return pl.pallas_call(
        _inproj_kernel,
        grid=(r // tm, ncols),
        in_specs=[pl.BlockSpec((tm, d), lambda i, j: (i, 0)),
                  pl.BlockSpec((1, 1, d), sh_map),
                  pl.BlockSpec((1, 1, d), sc_map),
                  pl.BlockSpec((1, d), lambda i, j: (0, 0)),
                  pl.BlockSpec((d, tn), lambda i, j: (0, j))],
        out_specs=[pl.BlockSpec((tm, tn), lambda i, j: (i, jnp.minimum(j, ncols - 2))),
                   pl.BlockSpec((tm, tn), lambda i, j: (i, 0))],
        out_shape=[jax.ShapeDtypeStruct((r, (ncols - 1) * tn), BF16),
                   jax.ShapeDtypeStruct((r, tn), F32)],
        scratch_shapes=[pltpu.VMEM((tm, d), BF16)],
        name="inproj",
        compiler_params=_params(("arbitrary", "arbitrary")),
    )(x2d, mod3, mod3, g, w16)


def _pool_kernel(p_ref, w_ref, sc_ref, o_ref, pad_scr, tmp_scr):
    gw, pp = GRID_W, POOL_PAD
    pad_scr[...] = jnp.zeros(pad_scr.shape, F32)
    r_i = lax.broadcasted_iota(I32, (gw, gw, GROUP), 0)
    c_i = lax.broadcasted_iota(I32, (gw, gw, GROUP), 1)
    for g, win in enumerate(POOL_WINDOWS):
        half = win // 2
        pg = p_ref[:, g * GROUP:(g + 1) * GROUP].astype(F32).reshape(gw, gw, GROUP)
        pad_scr[pp:pp + gw, pp:pp + gw, :] = pg
        acc = None
        for k in range(-half, win - half):
            sl = pad_scr[pp + k:pp + k + gw, :, :]
            acc = sl if acc is None else acc + sl
        tmp_scr[...] = acc
        acc = None
        for k in range(-half, win - half):
            sl = tmp_scr[:, pp + k:pp + k + gw, :]
            acc = sl if acc is None else acc + sl
        cr = jnp.minimum(r_i - half + win, gw) - jnp.maximum(r_i - half, 0)
        cc = jnp.minimum(c_i - half + win, gw) - jnp.maximum(c_i - half, 0)
        mean = acc / (cr * cc).astype(F32)
        d = (mean - pg).reshape(gw * gw, GROUP)
        y = _dot(d.astype(BF16), w_ref[g].astype(BF16)) * sc_ref[:, g * GROUP:(g + 1) * GROUP]
        o_ref[:, g * GROUP:(g + 1) * GROUP] = y.astype(BF16)


def _pool_call(z, pool_w, pool_scale, batch, seq):
    width = len(POOL_WINDOWS) * GROUP
    ext = GRID_W + 2 * POOL_PAD
    return pl.pallas_call(
        _pool_kernel,
        grid=(batch,),
        in_specs=[pl.BlockSpec((seq, width), lambda b: (b, 0)),
                  pl.BlockSpec(pool_w.shape, lambda b: (0, 0, 0)),
                  pl.BlockSpec((1, width), lambda b: (0, 0))],
        out_specs=pl.BlockSpec((seq, width), lambda b: (b, 0)),
        out_shape=jax.ShapeDtypeStruct((batch * seq, width), BF16),
        scratch_shapes=[pltpu.VMEM((ext, ext, GROUP), F32), pltpu.VMEM((GRID_W, ext, GROUP), F32)],
        name="pool",
        compiler_params=_params(("arbitrary",)),
    )(z, pool_w, pool_scale)


def _hgrn_masks(rev):
    t, c = HG_BLOCK, HG_CHUNK
    row = lax.broadcasted_iota(I32, (t, t), 0)
    col = lax.broadcasted_iota(I32, (t, t), 1)
    sh = c.bit_length() - 1
    same = (row >> sh) == (col >> sh)
    tri = (col >= row) if rev else (col <= row)
    mid = ((row >> sh) << sh) + HG_MID
    to_mid = (col >= mid) if rev else (col <= mid)
    m_all = jnp.concatenate([jnp.where(same & tri, 1.0, 0.0),
                             jnp.where(same, 1.0, 0.0),
                             jnp.where(same & to_mid, 1.0, 0.0)], axis=0).astype(BF16)
    return m_all, same & tri, col >> sh


def _hgrn_blocks(chains, masks, emit_out):
    t, c = HG_BLOCK, HG_CHUNK
    nch = t // c
    nc = len(chains)
    revs = [ch[0] for ch in chains]
    qs = [ch[2] for ch in chains]
    vs = [ch[3] for ch in chains]
    sts = [ch[5] for ch in chains]
    ks, bs = [], []
    for rev, x_f, _, _, lb, _ in chains:
        f = lb + (1.0 - lb) * _sigmoid(x_f)
        g_hi, g_lo = _hi_lo(jnp.log(f))
        m_all = masks[rev][0]
        ks.append(1.0 - f)
        bs.append(_dot(m_all, g_hi) + _dot(m_all, g_lo))
    b_loc = [b[0:t] for b in bs]
    b_tot = [b[t:2 * t] for b in bs]
    b_mid = [b[2 * t:3 * t] for b in bs]
    kd16 = [(ks[i] * jnp.exp(b_tot[i] - b_loc[i])).astype(BF16) for i in range(nc)]
    vt = [v.T for v in vs]
    a_all = [jnp.exp(b) for b in b_tot]
    os_ = [None] * nc
    if emit_out:
        a16 = [(qs[i] * jnp.exp(b_loc[i])).astype(BF16) for i in range(nc)]
        qd16 = [(qs[i] * jnp.exp(b_loc[i] - b_mid[i])).astype(BF16) for i in range(nc)]
        kg16 = [(ks[i] * jnp.exp(b_mid[i] - b_loc[i])).astype(BF16) for i in range(nc)]
        attn = [jnp.where(masks[revs[i]][1], _dot_nt(qd16[i], kg16[i]), 0.0) for i in range(nc)]
        os_ = [_dot(attn[i].astype(BF16), vs[i].astype(BF16)) for i in range(nc)]
        o_inter = [[None] * nch for _ in range(nc)]
    for step in range(nch):
        for i in range(nc):
            n = nch - 1 - step if revs[i] else step
            if emit_out:
                o_inter[i][n] = _dot_nt(a16[i][n * c:(n + 1) * c], sts[i].astype(BF16))
            ut = _dot(jnp.where(masks[revs[i]][2] == n, vt[i], 0.0).astype(BF16), kd16[i])
            sts[i] = sts[i] * a_all[i][n * c:n * c + 1, :] + ut
    if emit_out:
        os_ = [os_[i] + jnp.concatenate(o_inter[i], axis=0) for i in range(nc)]
    return list(zip(os_, sts))


def _lower_bounds(lbl_ref):
    out = []
    for d in range(2):
        l0, l1 = lbl_ref[2 * d:2 * d + 1, :], lbl_ref[2 * d + 1:2 * d + 2, :]
        m = jnp.maximum(l0, l1)
        e0, e1 = jnp.exp(l0 - m), jnp.exp(l1 - m)
        out.append(e0 / (e0 + e1))
    return out


def _hgrn_run(nheads, seq, lbl_ref, f_refs, q_ref, v_ref, st_scr, o_scr):
    t = HG_BLOCK
    nblk = seq // t
    masks = (_hgrn_masks(False), _hgrn_masks(True))
    lbs = _lower_bounds(lbl_ref)
    emit_out = o_scr is not None

    def body(i, carry, accumulate):
        chains = []
        for d in range(2):
            blk = (nblk - 1 - i) if d else i
            rows = pl.ds(pl.multiple_of(blk * t, t), t)
            for h in range(nheads):
                ls = slice(h * GROUP, (h + 1) * GROUP)
                q = q_ref[rows, ls].astype(F32) if emit_out else None
                chains.append((d, h, rows, ls, f_refs[d][rows, ls], q, v_ref[rows, ls].astype(F32),
                               st_scr[d * nheads + h]))
        results = _hgrn_blocks([(d, x_f, q, v, lbs[d][:, ls], st) for d, h, rows, ls, x_f, q, v, st in chains],
                               masks, emit_out)
        for (d, h, rows, ls, *_), (o, st) in zip(chains, results):
            st_scr[d * nheads + h] = st
            if emit_out and accumulate:
                o_scr[rows, ls] += o
            elif emit_out:
                o_scr[rows, ls] = o
        return carry

    if emit_out:
        assert nblk % 2 == 0
        lax.fori_loop(0, nblk // 2, functools.partial(body, accumulate=False), 0)
        lax.fori_loop(nblk // 2, nblk, functools.partial(body, accumulate=True), 0)
    else:
        lax.fori_loop(0, nblk, functools.partial(body, accumulate=False), 0)


def _hgrn_ctx_kernel(v_ref, ff_ref, fb_ref, lbl_ref, sf_ref, sb_ref, st_scr, *, seq):
    nheads = sf_ref.shape[1]
    st_scr[...] = jnp.zeros(st_scr.shape, F32)
    _hgrn_run(nheads, seq, lbl_ref, (ff_ref, fb_ref), None, v_ref, st_scr, None)
    sf_ref[0] = st_scr[0:nheads]
    sb_ref[0] = st_scr[nheads:2 * nheads]


def _hgrn_kernel(q_ref, v_ref, ff_ref, fb_ref, og_ref, lbl_ref, ng_ref, sf_ref, sb_ref, y_ref,
                 st_scr, o_scr, *, seq):
    nheads = sf_ref.shape[1]
    st_scr[0:nheads] = sf_ref[0]
    st_scr[nheads:2 * nheads] = sb_ref[0]
    _hgrn_run(nheads, seq, lbl_ref, (ff_ref, fb_ref), q_ref, v_ref, st_scr, o_scr)
    rb = 4 * HG_BLOCK

    def epilogue(i, carry):
        rows = pl.ds(pl.multiple_of(i * rb, rb), rb)
        for h in range(nheads):
            ls = slice(h * GROUP, (h + 1) * GROUP)
            og = og_ref[rows, ls].astype(F32)
            y = _rms(o_scr[rows, ls]) * ng_ref[...] * (og * _sigmoid(og))
            y_ref[rows, ls] = y.astype(BF16)
        return carry

    lax.fori_loop(0, seq // rb, epilogue, 0)


def _hgrn_specs(seq):
    hw = HG_HEADS * GROUP
    big = lambda k: pl.BlockSpec((seq, hw), lambda b: (b, k), pipeline_mode=pl.Buffered(1))
    st_spec = pl.BlockSpec((1, HG_HEADS, GROUP, GROUP), lambda b: (b, 0, 0, 0))
    return hw, big, st_spec


def _hgrn_ctx_call(zc16, zcf, lbl, batch, seq):
    hw, big, st_spec = _hgrn_specs(seq)
    st_shape = jax.ShapeDtypeStruct((batch, HG_HEADS, GROUP, GROUP), F32)
    return pl.pallas_call(
        functools.partial(_hgrn_ctx_kernel, seq=seq),
        grid=(batch,),
        in_specs=[big(0), big(0), big(1), pl.BlockSpec((4, hw), lambda b: (0, 0))],
        out_specs=[st_spec, st_spec],
        out_shape=[st_shape, st_shape],
        scratch_shapes=[pltpu.VMEM((2 * HG_HEADS, GROUP, GROUP), F32)],
        name="hgrn_ctx",
        compiler_params=_params(("arbitrary",)),
    )(zc16, zcf, zcf, lbl)


def _hgrn_call(z16, zf, lbl, norm_g, sf, sb, batch, seq):
    hw, big, st_spec = _hgrn_specs(seq)
    return pl.pallas_call(
        functools.partial(_hgrn_kernel, seq=seq),
        grid=(batch,),
        in_specs=[big(1), big(2), big(0), big(1), big(3),
                  pl.BlockSpec((4, hw), lambda b: (0, 0)),
                  pl.BlockSpec((1, GROUP), lambda b: (0, 0)),
                  st_spec, st_spec],
        out_specs=pl.BlockSpec((seq, hw), lambda b: (b, 0)),
        out_shape=jax.ShapeDtypeStruct((batch * seq, hw), BF16),
        scratch_shapes=[pltpu.VMEM((2 * HG_HEADS, GROUP, GROUP), F32), pltpu.VMEM((seq, hw), F32)],
        name="hgrn",
        compiler_params=_params(("arbitrary",)),
    )(z16, z16, zf, zf, z16, lbl, norm_g, sf, sb)


def _merge_kernel(x_ref, yp_ref, yh_ref, gp_ref, gh_ref, g1_ref, sh_ref, sc_ref, ng_ref,
                  wa_ref, wb_ref, wo_ref, wqt_ref, keys_ref, x1_ref, h2_ref, st_ref):
    m = (_sigmoid(gp_ref[...].astype(F32)) * _dot(yp_ref[...], wa_ref[...])
         + _sigmoid(gh_ref[...].astype(F32)) * _dot(yh_ref[...], wb_ref[...]))
    x1 = x_ref[...] + g1_ref[0] * _dot(m.astype(BF16), wo_ref[...])
    x1_ref[...] = x1
    h2 = _rms(x1) * ng_ref[...] * (1.0 + sc_ref[0]) + sh_ref[0]
    h2_ref[...] = (h2 * _SQRT_HALF).astype(BF16)
    qt = _dot(wqt_ref[...], h2.T.astype(BF16))
    for hp in range(2 * PEER_HEADS):
        st_ref[hp] = _dot(keys_ref[hp], qt[hp * GROUP:(hp + 1) * GROUP].astype(BF16))


def _merge_call(x2d, ypool, yhg, z, mod3, norm2_g, wa16, wb16, wo16, wqt16, keys16, seq, tm):
    r, d = x2d.shape
    gcol0 = (z.shape[1] - 2 * d) // d
    modrow = lambda k: (lambda i: (k + N_MOD * ((i * tm) // seq), 0, 0))
    const2 = lambda i: (0, 0)
    nhp = 2 * PEER_HEADS
    return pl.pallas_call(
        _merge_kernel,
        grid=(r // tm,),
        in_specs=[pl.BlockSpec((tm, d), lambda i: (i, 0)),
                  pl.BlockSpec((tm, ypool.shape[1]), lambda i: (i, 0)),
                  pl.BlockSpec((tm, yhg.shape[1]), lambda i: (i, 0)),
                  pl.BlockSpec((tm, d), lambda i: (i, gcol0)),
                  pl.BlockSpec((tm, d), lambda i: (i, gcol0 + 1)),
                  pl.BlockSpec((1, 1, d), modrow(2)),
                  pl.BlockSpec((1, 1, d), modrow(3)),
                  pl.BlockSpec((1, 1, d), modrow(4)),
                  pl.BlockSpec((1, d), const2),
                  pl.BlockSpec(wa16.shape, const2),
                  pl.BlockSpec(wb16.shape, const2),
                  pl.BlockSpec(wo16.shape, const2),
                  pl.BlockSpec(wqt16.shape, const2),
                  pl.BlockSpec(keys16.shape, lambda i: (0, 0, 0))],
        out_specs=[pl.BlockSpec((tm, d), lambda i: (i, 0)),
                   pl.BlockSpec((tm, d), lambda i: (i, 0)),
                   pl.BlockSpec((nhp, PEER_NKEYS, tm), lambda i: (0, 0, i))],
        out_shape=[jax.ShapeDtypeStruct((r, d), F32),
                   jax.ShapeDtypeStruct((r, d), BF16),
                   jax.ShapeDtypeStruct((nhp, PEER_NKEYS, r), F32)],
        name="merge",
        compiler_params=_params(("arbitrary",)),
    )(x2d, ypool, yhg, z, z, mod3, mod3, mod3, norm2_g, wa16, wb16, wo16, wqt16, keys16)


def _top16(v, iota_n):
    vals, idxs = [], []
    for _ in range(PEER_TOPK):
        m = jnp.max(v, axis=0, keepdims=True)
        idx = jnp.min(jnp.where(v == m, iota_n, float(PEER_NKEYS)), axis=0, keepdims=True)
        v = jnp.where(iota_n == idx, -jnp.inf, v)
        vals.append(m)
        idxs.append(idx)
    return jnp.concatenate(vals, axis=0), jnp.concatenate(idxs, axis=0)


def _topk_unit(s1, s2):
    tg = s1.shape[1]
    kk = PEER_TOPK
    nkf = float(PEER_NKEYS)
    iota_n = lax.broadcasted_iota(I32, (PEER_NKEYS, tg), 0).astype(F32)
    v1, n1 = _top16(s1, iota_n)
    v2, n2 = _top16(s2, iota_n)

    r = lax.broadcasted_iota(I32, (8, tg), 0)
    rf = r.astype(F32)
    sel3 = lambda x: jnp.where(r < 5, x[0], x[1])
    sel4 = lambda x: jnp.where(r < 4, x[0], jnp.where(r < 6, x[1], x[2]))
    sel5 = lambda x: jnp.where(r < 2, x[0], x[1])

    def tiles(l1, l2):
        row1 = lambda a: l1[a:a + 1]
        lo2, hi1 = l2[0:8], l1[8:16]
        hi1_r2 = pltpu.roll(hi1, 2, 0)
        t1 = [row1(0), row1(0), row1(1), sel3((row1(2), row1(4))), sel4((row1(3), row1(5), row1(6))),
              sel5((row1(7), hi1_r2)), hi1_r2]
        t2 = [lo2, l2[8:16], lo2, sel3((lo2, pltpu.roll(lo2, 5, 0))),
              sel4((lo2, pltpu.roll(lo2, 4, 0), pltpu.roll(lo2, 6, 0))), sel5((lo2, l2[0:1])), l2[0:1]]
        return t1, t2

    tv1, tv2 = tiles(v1, v2)
    tn1, tn2 = tiles(n1, n2)
    cand = [a + b for a, b in zip(tv1, tv2)]
    cand[6] = jnp.where(r < 2, cand[6], -jnp.inf)
    cand = jnp.concatenate(cand, axis=0)
    eid = jnp.concatenate([a * nkf + b for a, b in zip(tn1, tn2)], axis=0)
    pos = jnp.concatenate([rf, rf + 8.0, rf + 16.0, sel3((rf + 32.0, rf + 59.0)),
                           sel4((rf + 48.0, rf + 76.0, rf + 90.0)), sel5((rf + 112.0, rf * 16.0 + 96.0)),
                           rf * 16.0 + 224.0], axis=0)
    tops, es = [], []
    for _ in range(kk):
        m = jnp.max(cand, axis=0, keepdims=True)
        p = jnp.min(jnp.where(cand == m, pos, 1e9), axis=0, keepdims=True)
        sel = pos == p
        es.append(jnp.max(jnp.where(sel, eid, -1.0), axis=0, keepdims=True))
        cand = jnp.where(sel, -jnp.inf, cand)
        tops.append(m)
    top = jnp.concatenate(tops, axis=0)
    e = jnp.concatenate(es, axis=0)
    ex = jnp.exp(top - top[0:1])
    gate = ex / jnp.sum(ex, axis=0, keepdims=True)
    e1 = jnp.floor(e * (1.0 / PEER_NKEYS))
    return e1, e - e1 * nkf, gate * _SQRT_HALF


def _peer_kernel(s_ref, h2_ref, ut_ref, v_ref, x1_ref, g2_ref, fg_ref, o_ref,
                 w_scr, stage_scr, tr0_scr, tr1_scr, acc_scr, rt_scr, rr_scr):
    tb, d = h2_ref.shape
    eb = v_ref.shape[0]
    i = pl.program_id(0)
    e = pl.program_id(1)
    nk = PEER_NKEYS
    ngrp = tb // GROUP
    pack = w_scr.shape[2]
    unroll = stage_scr.shape[0]
    tr_scrs = (tr0_scr, tr1_scr)
    units = s_ref.shape[0] // 2
    steps_per_grp = PEER_HEADS // units

    @pl.when((i > 0) & (e == 0))
    def _build():
        rd = lax.rem(i + 1, 2)
        for fld in range(3):
            for g in range(ngrp):
                rr_scr[fld, g * GROUP:(g + 1) * GROUP, :] = rt_scr[rd, fld, g].T
        iota_s = lax.broadcasted_iota(I32, (nk, nk), 0).astype(F32).astype(BF16)
        zero = jnp.zeros((), BF16)
        one = jnp.ones((), BF16)

        def flush(tt):
            for g in range(unroll // pack):
                blk = stage_scr[g * pack:(g + 1) * pack].astype(BF16)
                tr_scrs[g][...] = jnp.transpose(blk, (1, 0, 2))
            for g in range(unroll // pack):
                w_scr[tt * (unroll // pack) + g] = tr_scrs[g][...]

        def body(tt, carry):
            flush(jnp.maximum(tt - 1, 0))
            ts = [tt * unroll + r for r in range(unroll)]
            row = lambda fld, t: jnp.broadcast_to(rr_scr[fld, pl.ds(t, 1), :], (pack, nk)).astype(BF16)
            tiles = [iota_s[v * pack:(v + 1) * pack] for v in range(nk // pack)]
            onehot = lambda idx, val: jnp.concatenate([jnp.where(tl == idx, val, zero) for tl in tiles], axis=0)
            g1t = [onehot(row(0, t), row(2, t)) for t in ts]
            p2t = [onehot(row(1, t), one) for t in ts]
            for r, (g, p) in enumerate(zip(g1t, p2t)):
                stage_scr[r] = _dot_nt(g, p)
            return carry

        stage_scr[...] = jnp.zeros(stage_scr.shape, F32)
        ntrip = tb // unroll
        lax.fori_loop(0, ntrip, body, 0)
        flush(ntrip - 1)
        acc_scr[...] = jnp.zeros(acc_scr.shape, F32)

    grp = e // steps_per_grp
    head0 = lax.rem(e, steps_per_grp) * units
    wr = lax.rem(i, 2)
    n1_0 = e * (eb // nk)
    nchunk = eb // (2 * nk)

    def route(k):
        i1, i2, gate = _topk_unit(s_ref[2 * k], s_ref[2 * k + 1])
        rows = pl.ds(pl.multiple_of((head0 + k) * PEER_TOPK, PEER_TOPK), PEER_TOPK)
        rt_scr[wr, 0, grp, rows, :] = i1
        rt_scr[wr, 1, grp, rows, :] = i2
        rt_scr[wr, 2, grp, rows, :] = gate

    def activate(k2, pieces):
        a2 = _dot(h2_ref[...], ut_ref[:, 2 * k2 * nk:2 * (k2 + 1) * nk])
        for k in (2 * k2, 2 * k2 + 1):
            ak = a2[:, (k - 2 * k2) * nk:(k - 2 * k2 + 1) * nk]
            wn = w_scr[:, n1_0 + k, :, :].reshape(tb, nk)
            act = ak * (1.0 + lax.erf(ak))
            pieces.append(wn * act.astype(BF16))

    @pl.when(i == 0)
    def _route_only():
        for k in range(units):
            route(k)

    @pl.when(i > 0)
    def _route_and_evaluate():
        pieces = []
        per = nchunk // units
        for k in range(units):
            route(k)
            for k2 in range(k * per, (k + 1) * per):
                activate(k2, pieces)
        wg = jnp.concatenate(pieces, axis=1)
        tn = 256
        for j in range(d // tn):
            acc_scr[:, j * tn:(j + 1) * tn] += _dot(wg, v_ref[:, j * tn:(j + 1) * tn])

    @pl.when((i > 0) & (e == pl.num_programs(1) - 1))
    def _fin():
        x2 = x1_ref[...] + g2_ref[0] * acc_scr[...]
        o_ref[...] = _rms(x2) * fg_ref[...]


def _peer_call(st, h2, ut16, v16, x1, mod3, final_g, seq, tb, eb):
    r, d = h2.shape
    ne = v16.shape[0]
    nblk = r // tb
    ngrp = tb // GROUP
    nsteps = ne // eb
    units = PEER_HEADS * ngrp // nsteps
    assert units * nsteps == PEER_HEADS * ngrp and PEER_HEADS % units == 0
    steps_per_grp = PEER_HEADS // units
    slots = PEER_HEADS * PEER_TOPK
    pack = 16
    prev = lambda i: jnp.maximum(i - 1, 0)
    tok = lambda i, e: (prev(i), 0)
    return pl.pallas_call(
        _peer_kernel,
        grid=(nblk + 1, nsteps),
        in_specs=[pl.BlockSpec((2 * units, PEER_NKEYS, GROUP),
                               lambda i, e: (lax.rem(e, steps_per_grp), 0,
                                             jnp.minimum(i, nblk - 1) * ngrp + e // steps_per_grp)),
                  pl.BlockSpec((tb, d), tok),
                  pl.BlockSpec((d, eb), lambda i, e: (0, e)),
                  pl.BlockSpec((eb, d), lambda i, e: (e, 0)),
                  pl.BlockSpec((tb, d), tok),
                  pl.BlockSpec((1, 1, d), lambda i, e: (5 + N_MOD * ((prev(i) * tb) // seq), 0, 0)),
                  pl.BlockSpec((1, d), lambda i, e: (0, 0))],
        out_specs=pl.BlockSpec((tb, d), tok),
        out_shape=jax.ShapeDtypeStruct((r, d), F32),
        scratch_shapes=[pltpu.VMEM((tb // pack, PEER_NKEYS, pack, PEER_NKEYS), BF16),
                        pltpu.VMEM((2 * pack, PEER_NKEYS, PEER_NKEYS), F32),
                        pltpu.VMEM((PEER_NKEYS, pack, PEER_NKEYS), BF16),
                        pltpu.VMEM((PEER_NKEYS, pack, PEER_NKEYS), BF16),
                        pltpu.VMEM((tb, d), F32),
                        pltpu.VMEM((2, 3, ngrp, slots, GROUP), F32),
                        pltpu.VMEM((3, tb, slots), F32)],
        name="peer",
        compiler_params=_params(("arbitrary", "arbitrary")),
    )(st, h2, ut16, v16, x1, mod3, final_g)


def kernel(x, c, ctx, c_ctx, w_mod, b_mod, norm1_g, norm2_g, w_in, pool_w, pool_scale, hg_lb_logits, hg_norm_g,
           w_a, w_b, w_o, peer_wq, peer_keys, peer_u, peer_v, final_g):
    batch, seq, d = x.shape
    ctx_len = ctx.shape[1]
    assert w_mod.shape[0] == 1 and seq == GRID_W * GRID_W and hg_lb_logits.shape[1] == 2
    hw = HG_HEADS * GROUP

    cvec = jnp.concatenate([c, c_ctx[None, :], jnp.zeros((8 - batch - 1, d), F32)], axis=0)
    mod = _mod_call(cvec, w_mod[0], b_mod[0][None, :])
    mod3 = mod.reshape(8 * N_MOD, 1, d)

    w_in16 = w_in[0].astype(BF16)
    f0, f1 = 3 * hw, 5 * hw
    w_in16 = jnp.concatenate([w_in16[:, :f0], w_in16[:, f1:], w_in16[:, f0:f1]], axis=1)
    g1n = norm1_g[0][None, :]
    x2d = x.reshape(batch * seq, d)
    z16, zf = _inproj_call(x2d, mod3, 0, 1, seq, g1n, w_in16, TM_INPROJ, 2 * hw)
    w_ctx16 = jnp.concatenate([w_in16[:, 2 * hw:4 * hw], w_in16[:, -2 * hw:]], axis=1)
    zc16, zcf = _inproj_call(ctx.reshape(batch * ctx_len, d), mod3, N_MOD * batch, N_MOD * batch + 1, None, g1n,
                             w_ctx16, TM_CTX, 2 * hw)

    lbl = hg_lb_logits.reshape(2 * hg_lb_logits.shape[1], hw)
    sf, sb = _hgrn_ctx_call(zc16, zcf, lbl, batch, ctx_len)
    yhg = _hgrn_call(z16, zf, lbl, hg_norm_g[0][None, :], sf, sb, batch, seq)
    ypool = _pool_call(z16, pool_w[0], pool_scale[0][None, :], batch, seq)

    nhp = 2 * PEER_HEADS
    x1, h2, st = _merge_call(x2d, ypool, yhg, z16, mod3, norm2_g[0][None, :],
                             w_a[0].astype(BF16), w_b[0].astype(BF16), w_o[0].astype(BF16),
                             peer_wq[0].T.astype(BF16),
                             peer_keys[0].reshape(nhp, PEER_NKEYS, GROUP).astype(BF16), seq, TM_MERGE)
    out = _peer_call(st, h2, peer_u[0].T.astype(BF16), peer_v[0].astype(BF16), x1, mod3,
                     final_g[None, :], seq, TB_PEER, EB_PEER)
    return out.reshape(batch, seq, d)
```

```python
import functools
import math

import jax
import jax.numpy as jnp
from jax import lax
from jax.experimental import pallas as pl
from jax.experimental.pallas import tpu as pltpu

F32 = jnp.float32
BF16 = jnp.bfloat16
I32 = jnp.int32

EPS = 1e-6
GRID_W = 64
POOL_WINDOWS = (2, 4, 8, 16)
POOL_PAD = 8
GROUP = 128
HG_HEADS = 4
HG_CHUNK = 16
HG_BLOCK = 128
HG_MID = 8
PEER_HEADS = 8
PEER_NKEYS = 128
PEER_TOPK = 16
N_MOD = 6
VMEM_LIMIT = 60 * 1024 * 1024
TM_INPROJ = 2048
TM_CTX = 256
TM_MERGE = 512
TB_PEER = 512
EB_PEER = 2048

_NT = (((1,), (1,)), ((), ()))
_SQRT_HALF = math.sqrt(0.5)


def _dot(a, b):
    return jnp.dot(a, b, preferred_element_type=F32)


def _dot_nt(a, b):
    return lax.dot_general(a, b, _NT, preferred_element_type=F32)


def _hi_lo(a):
    hi = a.astype(BF16)
    return hi, (a - hi.astype(F32)).astype(BF16)


def _sigmoid(x):
    return 1.0 / (1.0 + jnp.exp(-x))


def _rms(x):
    return x * lax.rsqrt(jnp.mean(x * x, axis=-1, keepdims=True) + EPS)


def _params(sem):
    return pltpu.CompilerParams(dimension_semantics=sem, vmem_limit_bytes=VMEM_LIMIT)


def _mod_kernel(c_ref, w_ref, b_ref, o_ref):
    c = c_ref[...]
    s = c * _sigmoid(c)
    s_hi, s_lo = _hi_lo(s)
    w_hi, w_lo = _hi_lo(w_ref[...])
    o_ref[...] = _dot(s_hi, w_hi) + (_dot(s_hi, w_lo) + _dot(s_lo, w_hi)) + b_ref[...]


def _mod_call(cvec, w_mod, b_mod):
    rows, d = cvec.shape
    n = w_mod.shape[1]
    tn = n // 4
    return pl.pallas_call(
        _mod_kernel,
        grid=(n // tn,),
        in_specs=[pl.BlockSpec((rows, d), lambda j: (0, 0)),
                  pl.BlockSpec((d, tn), lambda j: (0, j)),
                  pl.BlockSpec((1, tn), lambda j: (0, j))],
        out_specs=pl.BlockSpec((rows, tn), lambda j: (0, j)),
        out_shape=jax.ShapeDtypeStruct((rows, n), F32),
        name="mod",
        compiler_params=_params(("arbitrary",)),
    )(cvec, w_mod, b_mod)


def _inproj_kernel(x_ref, sh_ref, sc_ref, g_ref, w_ref, o16_ref, of_ref, h_scr):
    j = pl.program_id(1)
    last = pl.num_programs(1) - 1

    @pl.when(j == 0)
    def _():
        h = _rms(x_ref[...]) * g_ref[...] * (1.0 + sc_ref[0]) + sh_ref[0]
        h_scr[...] = h.astype(BF16)

    y = _dot(h_scr[...], w_ref[...])

    @pl.when(j < last)
    def _():
        o16_ref[...] = y.astype(BF16)

    @pl.when(j == last)
    def _():
        of_ref[...] = y


def _inproj_call(x2d, mod3, sh_row, sc_row, rows_per_mod, g, w16, tm, tn):
    r, d = x2d.shape
    ncols = w16.shape[1] // tn
    if rows_per_mod is None:
        sh_map = lambda i, j: (sh_row, 0, 0)
        sc_map = lambda i, j: (sc_row, 0, 0)
    else:
        sh_map = lambda i, j: (sh_row + N_MOD * ((i * tm) // rows_per_mod), 0, 0)
        sc_map = lambda i, j: (sc_row + N_MOD * ((i * tm) // rows_per_mod), 0, 0)
    return pl.pallas_call(
        _inproj_kernel,
        grid=(r // tm, ncols),
        in_specs=[pl.BlockSpec((tm, d), lambda i, j: (i, 0)),
                  pl.BlockSpec((1, 1, d), sh_map),
                  pl.BlockSpec((1, 1, d), sc_map),
                  pl.BlockSpec((1, d), lambda i, j: (0, 0)),
                  pl.BlockSpec((d, tn), lambda i, j: (0, j))],
        out_specs=[pl.BlockSpec((tm, tn), lambda i, j: (i, jnp.minimum(j, ncols - 2))),
                   pl.BlockSpec((tm, tn), lambda i, j: (i, 0))],
        out_shape=[jax.ShapeDtypeStruct((r, (ncols - 1) * tn), BF16),
                   jax.ShapeDtypeStruct((r, tn), F32)],
        scratch_shapes=[pltpu.VMEM((tm, d), BF16)],
        name="inproj",
        compiler_params=_params(("arbitrary", "arbitrary")),
    )(x2d, mod3, mod3, g, w16)


def _pool_kernel(p_ref, w_ref, sc_ref, o_ref, pad_scr, tmp_scr):
    gw, pp = GRID_W, POOL_PAD
    pad_scr[...] = jnp.zeros(pad_scr.shape, F32)
    r_i = lax.broadcasted_iota(I32, (gw, gw, GROUP), 0)
    c_i = lax.broadcasted_iota(I32, (gw, gw, GROUP), 1)
    for g, win in enumerate(POOL_WINDOWS):
        half = win // 2
        pg = p_ref[:, g * GROUP:(g + 1) * GROUP].astype(F32).reshape(gw, gw, GROUP)
        pad_scr[pp:pp + gw, pp:pp + gw, :] = pg
        acc = None
        for k in range(-half, win - half):
            sl = pad_scr[pp + k:pp + k + gw, :, :]
            acc = sl if acc is None else acc + sl
        tmp_scr[...] = acc
        acc = None
        for k in range(-half, win - half):
            sl = tmp_scr[:, pp + k:pp + k + gw, :]
            acc = sl if acc is None else acc + sl
        cr = jnp.minimum(r_i - half + win, gw) - jnp.maximum(r_i - half, 0)
        cc = jnp.minimum(c_i - half + win, gw) - jnp.maximum(c_i - half, 0)
        mean = acc / (cr * cc).astype(F32)
        d = (mean - pg).reshape(gw * gw, GROUP)
        y = _dot(d.astype(BF16), w_ref[g].astype(BF16)) * sc_ref[:, g * GROUP:(g + 1) * GROUP]
        o_ref[:, g * GROUP:(g + 1) * GROUP] = y.astype(BF16)


def _pool_call(z, pool_w, pool_scale, batch, seq):
    width = len(POOL_WINDOWS) * GROUP
    ext = GRID_W + 2 * POOL_PAD
    return pl.pallas_call(
        _pool_kernel,
        grid=(batch,),
        in_specs=[pl.BlockSpec((seq, width), lambda b: (b, 0)),
                  pl.BlockSpec(pool_w.shape, lambda b: (0, 0, 0)),
                  pl.BlockSpec((1, width), lambda b: (0, 0))],
        out_specs=pl.BlockSpec((seq, width), lambda b: (b, 0)),
        out_shape=jax.ShapeDtypeStruct((batch * seq, width), BF16),
        scratch_shapes=[pltpu.VMEM((ext, ext, GROUP), F32), pltpu.VMEM((GRID_W, ext, GROUP), F32)],
        name="pool",
        compiler_params=_params(("arbitrary",)),
    )(z, pool_w, pool_scale)


def _hgrn_masks(rev):
    t, c = HG_BLOCK, HG_CHUNK
    row = lax.broadcasted_iota(I32, (t, t), 0)
    col = lax.broadcasted_iota(I32, (t, t), 1)
    sh = c.bit_length() - 1
    same = (row >> sh) == (col >> sh)
    tri = (col >= row) if rev else (col <= row)
    mid = ((row >> sh) << sh) + HG_MID
    to_mid = (col >= mid) if rev else (col <= mid)
    m_all = jnp.concatenate([jnp.where(same & tri, 1.0, 0.0),
                             jnp.where(same, 1.0, 0.0),
                             jnp.where(same & to_mid, 1.0, 0.0)], axis=0).astype(BF16)
    return m_all, same & tri


def _hgrn_blocks(chains, masks, emit_out):
    t, c = HG_BLOCK, HG_CHUNK
    nch = t // c
    nc = len(chains)
    revs = [ch[0] for ch in chains]
    qs = [ch[2] for ch in chains]
    vs = [ch[3] for ch in chains]
    sts = [ch[5] for ch in chains]
    ks, bs = [], []
    for rev, x_f, _, _, lb, _ in chains:
        f = lb + (1.0 - lb) * _sigmoid(x_f)
        g_hi, g_lo = _hi_lo(jnp.log(f))
        m_all = masks[rev][0]
        ks.append(1.0 - f)
        bs.append(_dot(m_all, g_hi) + _dot(m_all, g_lo))
    b_loc = [b[0:t] for b in bs]
    b_tot = [b[t:2 * t] for b in bs]
    b_mid = [b[2 * t:3 * t] for b in bs]
    kd16 = [(ks[i] * jnp.exp(b_tot[i] - b_loc[i])).astype(BF16) for i in range(nc)]
    v16 = [v.astype(BF16) for v in vs]
    a_all = [jnp.exp(b) for b in b_tot]
    os_ = [None] * nc
    if emit_out:
        a16 = [(qs[i] * jnp.exp(b_loc[i])).astype(BF16) for i in range(nc)]
        qd16 = [(qs[i] * jnp.exp(b_loc[i] - b_mid[i])).astype(BF16) for i in range(nc)]
        kg16 = [(ks[i] * jnp.exp(b_mid[i] - b_loc[i])).astype(BF16) for i in range(nc)]
        attn = [jnp.where(masks[revs[i]][1], _dot_nt(qd16[i], kg16[i]), 0.0) for i in range(nc)]
        os_ = [_dot(attn[i].astype(BF16), v16[i]) for i in range(nc)]
        o_inter = [[None] * nch for _ in range(nc)]
    for step in range(nch):
        for i in range(nc):
            n = nch - 1 - step if revs[i] else step
            if emit_out:
                o_inter[i][n] = _dot_nt(a16[i][n * c:(n + 1) * c], sts[i].astype(BF16))
            ut = lax.dot_general(v16[i][n * c:(n + 1) * c], kd16[i][n * c:(n + 1) * c], (((0,), (0,)), ((), ())),
                                 preferred_element_type=F32)
            sts[i] = sts[i] * a_all[i][n * c:n * c + 1, :] + ut
    if emit_out:
        os_ = [os_[i] + jnp.concatenate(o_inter[i], axis=0) for i in range(nc)]
    return list(zip(os_, sts))


def _lower_bounds(lbl_ref):
    out = []
    for d in range(2):
        l0, l1 = lbl_ref[2 * d:2 * d + 1, :], lbl_ref[2 * d + 1:2 * d + 2, :]
        m = jnp.maximum(l0, l1)
        e0, e1 = jnp.exp(l0 - m), jnp.exp(l1 - m)
        out.append(e0 / (e0 + e1))
    return out


def _hgrn_run(nheads, seq, lbl_ref, f_refs, q_ref, v_ref, st_scr, o_scr):
    t = HG_BLOCK
    nblk = seq // t
    masks = (_hgrn_masks(False), _hgrn_masks(True))
    lbs = _lower_bounds(lbl_ref)
    emit_out = o_scr is not None

    def body(i, carry, accumulate):
        chains = []
        for d in range(2):
            blk = (nblk - 1 - i) if d else i
            rows = pl.ds(pl.multiple_of(blk * t, t), t)
            for h in range(nheads):
                ls = slice(h * GROUP, (h + 1) * GROUP)
                q = q_ref[rows, ls].astype(F32) if emit_out else None
                chains.append((d, h, rows, ls, f_refs[d][rows, ls], q, v_ref[rows, ls].astype(F32),
                               st_scr[d * nheads + h]))
        results = _hgrn_blocks([(d, x_f, q, v, lbs[d][:, ls], st) for d, h, rows, ls, x_f, q, v, st in chains],
                               masks, emit_out)
        for (d, h, rows, ls, *_), (o, st) in zip(chains, results):
            st_scr[d * nheads + h] = st
            if emit_out and accumulate:
                o_scr[rows, ls] += o
            elif emit_out:
                o_scr[rows, ls] = o
        return carry

    if emit_out:
        assert nblk % 2 == 0
        lax.fori_loop(0, nblk // 2, functools.partial(body, accumulate=False), 0)
        lax.fori_loop(nblk // 2, nblk, functools.partial(body, accumulate=True), 0)
    else:
        lax.fori_loop(0, nblk, functools.partial(body, accumulate=False), 0)


def _hgrn_ctx_kernel(v_ref, ff_ref, fb_ref, lbl_ref, sf_ref, sb_ref, st_scr, *, seq):
    nheads = sf_ref.shape[1]
    st_scr[...] = jnp.zeros(st_scr.shape, F32)
    _hgrn_run(nheads, seq, lbl_ref, (ff_ref, fb_ref), None, v_ref, st_scr, None)
    sf_ref[0] = st_scr[0:nheads]
    sb_ref[0] = st_scr[nheads:2 * nheads]


def _hgrn_kernel(q_ref, v_ref, ff_ref, fb_ref, og_ref, lbl_ref, ng_ref, sf_ref, sb_ref, y_ref,
                 st_scr, o_scr, *, seq):
    nheads = sf_ref.shape[1]
    st_scr[0:nheads] = sf_ref[0]
    st_scr[nheads:2 * nheads] = sb_ref[0]
    _hgrn_run(nheads, seq, lbl_ref, (ff_ref, fb_ref), q_ref, v_ref, st_scr, o_scr)
    rb = 4 * HG_BLOCK

    def epilogue(i, carry):
        rows = pl.ds(pl.multiple_of(i * rb, rb), rb)
        for h in range(nheads):
            ls = slice(h * GROUP, (h + 1) * GROUP)
            og = og_ref[rows, ls].astype(F32)
            y = _rms(o_scr[rows, ls]) * ng_ref[...] * (og * _sigmoid(og))
            y_ref[rows, ls] = y.astype(BF16)
        return carry

    lax.fori_loop(0, seq // rb, epilogue, 0)


def _hgrn_specs(seq):
    hw = HG_HEADS * GROUP
    big = lambda k: pl.BlockSpec((seq, hw), lambda b: (b, k), pipeline_mode=pl.Buffered(1))
    st_spec = pl.BlockSpec((1, HG_HEADS, GROUP, GROUP), lambda b: (b, 0, 0, 0))
    return hw, big, st_spec


def _hgrn_ctx_call(zc16, zcf, lbl, batch, seq):
    hw, big, st_spec = _hgrn_specs(seq)
    st_shape = jax.ShapeDtypeStruct((batch, HG_HEADS, GROUP, GROUP), F32)
    return pl.pallas_call(
        functools.partial(_hgrn_ctx_kernel, seq=seq),
        grid=(batch,),
        in_specs=[big(0), big(0), big(1), pl.BlockSpec((4, hw), lambda b: (0, 0))],
        out_specs=[st_spec, st_spec],
        out_shape=[st_shape, st_shape],
        scratch_shapes=[pltpu.VMEM((2 * HG_HEADS, GROUP, GROUP), F32)],
        name="hgrn_ctx",
        compiler_params=_params(("arbitrary",)),
    )(zc16, zcf, zcf, lbl)


def _hgrn_call(z16, zf, lbl, norm_g, sf, sb, batch, seq):
    hw, big, st_spec = _hgrn_specs(seq)
    return pl.pallas_call(
        functools.partial(_hgrn_kernel, seq=seq),
        grid=(batch,),
        in_specs=[big(1), big(2), big(0), big(1), big(3),
                  pl.BlockSpec((4, hw), lambda b: (0, 0)),
                  pl.BlockSpec((1, GROUP), lambda b: (0, 0)),
                  st_spec, st_spec],
        out_specs=pl.BlockSpec((seq, hw), lambda b: (b, 0)),
        out_shape=jax.ShapeDtypeStruct((batch * seq, hw), BF16),
        scratch_shapes=[pltpu.VMEM((2 * HG_HEADS, GROUP, GROUP), F32), pltpu.VMEM((seq, hw), F32)],
        name="hgrn",
        compiler_params=_params(("arbitrary",)),
    )(z16, z16, zf, zf, z16, lbl, norm_g, sf, sb)


def _merge_kernel(x_ref, yp_ref, yh_ref, gp_ref, gh_ref, g1_ref, sh_ref, sc_ref, ng_ref,
                  wa_ref, wb_ref, wo_ref, wqt_ref, keys_ref, x1_ref, h2_ref, st_ref):
    m = (_sigmoid(gp_ref[...].astype(F32)) * _dot(yp_ref[...], wa_ref[...])
         + _sigmoid(gh_ref[...].astype(F32)) * _dot(yh_ref[...], wb_ref[...]))
    x1 = x_ref[...] + g1_ref[0] * _dot(m.astype(BF16), wo_ref[...])
    x1_ref[...] = x1
    h2 = _rms(x1) * ng_ref[...] * (1.0 + sc_ref[0]) + sh_ref[0]
    h2_ref[...] = (h2 * _SQRT_HALF).astype(BF16)
    qt = _dot(wqt_ref[...], h2.T.astype(BF16))
    for hp in range(2 * PEER_HEADS):
        st_ref[hp] = _dot(keys_ref[hp], qt[hp * GROUP:(hp + 1) * GROUP].astype(BF16))


def _merge_call(x2d, ypool, yhg, z, mod3, norm2_g, wa16, wb16, wo16, wqt16, keys16, seq, tm):
    r, d = x2d.shape
    gcol0 = (z.shape[1] - 2 * d) // d
    modrow = lambda k: (lambda i: (k + N_MOD * ((i * tm) // seq), 0, 0))
    const2 = lambda i: (0, 0)
    nhp = 2 * PEER_HEADS
    return pl.pallas_call(
        _merge_kernel,
        grid=(r // tm,),
        in_specs=[pl.BlockSpec((tm, d), lambda i: (i, 0)),
                  pl.BlockSpec((tm, ypool.shape[1]), lambda i: (i, 0)),
                  pl.BlockSpec((tm, yhg.shape[1]), lambda i: (i, 0)),
                  pl.BlockSpec((tm, d), lambda i: (i, gcol0)),
                  pl.BlockSpec((tm, d), lambda i: (i, gcol0 + 1)),
                  pl.BlockSpec((1, 1, d), modrow(2)),
                  pl.BlockSpec((1, 1, d), modrow(3)),
                  pl.BlockSpec((1, 1, d), modrow(4)),
                  pl.BlockSpec((1, d), const2),
                  pl.BlockSpec(wa16.shape, const2),
                  pl.BlockSpec(wb16.shape, const2),
                  pl.BlockSpec(wo16.shape, const2),
                  pl.BlockSpec(wqt16.shape, const2),
                  pl.BlockSpec(keys16.shape, lambda i: (0, 0, 0))],
        out_specs=[pl.BlockSpec((tm, d), lambda i: (i, 0)),
                   pl.BlockSpec((tm, d), lambda i: (i, 0)),
                   pl.BlockSpec((nhp, PEER_NKEYS, tm), lambda i: (0, 0, i))],
        out_shape=[jax.ShapeDtypeStruct((r, d), F32),
                   jax.ShapeDtypeStruct((r, d), BF16),
                   jax.ShapeDtypeStruct((nhp, PEER_NKEYS, r), F32)],
        name="merge",
        compiler_params=_params(("arbitrary",)),
    )(x2d, ypool, yhg, z, z, mod3, mod3, mod3, norm2_g, wa16, wb16, wo16, wqt16, keys16)


def _top16(v, iota_n):
    vals, idxs = [], []
    for _ in range(PEER_TOPK):
        m = jnp.max(v, axis=0, keepdims=True)
        idx = jnp.min(jnp.where(v == m, iota_n, float(PEER_NKEYS)), axis=0, keepdims=True)
        v = jnp.where(iota_n == idx, -jnp.inf, v)
        vals.append(m)
        idxs.append(idx)
    return jnp.concatenate(vals, axis=0), jnp.concatenate(idxs, axis=0)


def _topk_unit(s1, s2):
    tg = s1.shape[1]
    kk = PEER_TOPK
    nkf = float(PEER_NKEYS)
    iota_n = lax.broadcasted_iota(I32, (PEER_NKEYS, tg), 0).astype(F32)
    v1, n1 = _top16(s1, iota_n)
    v2, n2 = _top16(s2, iota_n)

    r = lax.broadcasted_iota(I32, (8, tg), 0)
    rf = r.astype(F32)
    sel3 = lambda x: jnp.where(r < 5, x[0], x[1])
    sel4 = lambda x: jnp.where(r < 4, x[0], jnp.where(r < 6, x[1], x[2]))
    sel5 = lambda x: jnp.where(r < 2, x[0], x[1])

    def tiles(l1, l2):
        row1 = lambda a: l1[a:a + 1]
        lo2, hi1 = l2[0:8], l1[8:16]
        hi1_r2 = pltpu.roll(hi1, 2, 0)
        t1 = [row1(0), row1(0), row1(1), sel3((row1(2), row1(4))), sel4((row1(3), row1(5), row1(6))),
              sel5((row1(7), hi1_r2)), hi1_r2]
        t2 = [lo2, l2[8:16], lo2, sel3((lo2, pltpu.roll(lo2, 5, 0))),
              sel4((lo2, pltpu.roll(lo2, 4, 0), pltpu.roll(lo2, 6, 0))), sel5((lo2, l2[0:1])), l2[0:1]]
        return t1, t2

    tv1, tv2 = tiles(v1, v2)
    tn1, tn2 = tiles(n1, n2)
    cand = [a + b for a, b in zip(tv1, tv2)]
    cand[6] = jnp.where(r < 2, cand[6], -jnp.inf)
    cand = jnp.concatenate(cand, axis=0)
    eid = jnp.concatenate([a * nkf + b for a, b in zip(tn1, tn2)], axis=0)
    pos = jnp.concatenate([rf, rf + 8.0, rf + 16.0, sel3((rf + 32.0, rf + 59.0)),
                           sel4((rf + 48.0, rf + 76.0, rf + 90.0)), sel5((rf + 112.0, rf * 16.0 + 96.0)),
                           rf * 16.0 + 224.0], axis=0)
    tops, es = [], []
    for _ in range(kk):
        m = jnp.max(cand, axis=0, keepdims=True)
        p = jnp.min(jnp.where(cand == m, pos, 1e9), axis=0, keepdims=True)
        sel = pos == p
        es.append(jnp.max(jnp.where(sel, eid, -1.0), axis=0, keepdims=True))
        cand = jnp.where(sel, -jnp.inf, cand)
        tops.append(m)
    top = jnp.concatenate(tops, axis=0)
    e = jnp.concatenate(es, axis=0)
    ex = jnp.exp(top - top[0:1])
    gate = ex / jnp.sum(ex, axis=0, keepdims=True)
    e1 = jnp.floor(e * (1.0 / PEER_NKEYS))
    return e1, e - e1 * nkf, gate * _SQRT_HALF


def _peer_kernel(s_ref, h2_ref, ut_ref, v_ref, x1_ref, g2_ref, fg_ref, o_ref,
                 w_scr, stage_scr, tr0_scr, tr1_scr, acc_scr, rt_scr, rr_scr):
    tb, d = h2_ref.shape
    eb = v_ref.shape[0]
    i = pl.program_id(0)
    e = pl.program_id(1)
    nk = PEER_NKEYS
    ngrp = tb // GROUP
    pack = w_scr.shape[2]
    unroll = stage_scr.shape[0]
    tr_scrs = (tr0_scr, tr1_scr)
    units = s_ref.shape[0] // 2
    steps_per_grp = PEER_HEADS // units

    @pl.when((i > 0) & (e == 0))
    def _build():
        rd = lax.rem(i + 1, 2)
        for fld in range(3):
            for g in range(ngrp):
                rr_scr[fld, g * GROUP:(g + 1) * GROUP, :] = rt_scr[rd, fld, g].T
        iota_s = lax.broadcasted_iota(I32, (nk, nk), 0).astype(F32).astype(BF16)
        zero = jnp.zeros((), BF16)
        one = jnp.ones((), BF16)

        def flush(tt):
            for g in range(unroll // pack):
                blk = stage_scr[g * pack:(g + 1) * pack].astype(BF16)
                tr_scrs[g][...] = jnp.transpose(blk, (1, 0, 2))
            for g in range(unroll // pack):
                w_scr[tt * (unroll // pack) + g] = tr_scrs[g][...]

        def body(tt, carry):
            flush(jnp.maximum(tt - 1, 0))
            ts = [tt * unroll + r for r in range(unroll)]
            row = lambda fld, t: jnp.broadcast_to(rr_scr[fld, pl.ds(t, 1), :], (pack, nk)).astype(BF16)
            tiles = [iota_s[v * pack:(v + 1) * pack] for v in range(nk // pack)]
            onehot = lambda idx, val: jnp.concatenate([jnp.where(tl == idx, val, zero) for tl in tiles], axis=0)
            g1t = [onehot(row(0, t), row(2, t)) for t in ts]
            p2t = [onehot(row(1, t), one) for t in ts]
            for r, (g, p) in enumerate(zip(g1t, p2t)):
                stage_scr[r] = _dot_nt(g, p)
            return carry

        stage_scr[...] = jnp.zeros(stage_scr.shape, F32)
        ntrip = tb // unroll
        lax.fori_loop(0, ntrip, body, 0)
        flush(ntrip - 1)
        acc_scr[...] = jnp.zeros(acc_scr.shape, F32)

    grp = e // steps_per_grp
    head0 = lax.rem(e, steps_per_grp) * units
    wr = lax.rem(i, 2)
    n1_0 = e * (eb // nk)
    nchunk = eb // (2 * nk)

    def route(k):
        i1, i2, gate = _topk_unit(s_ref[2 * k], s_ref[2 * k + 1])
        rows = pl.ds(pl.multiple_of((head0 + k) * PEER_TOPK, PEER_TOPK), PEER_TOPK)
        rt_scr[wr, 0, grp, rows, :] = i1
        rt_scr[wr, 1, grp, rows, :] = i2
        rt_scr[wr, 2, grp, rows, :] = gate

    def activate(k2, pieces):
        a2 = _dot(h2_ref[...], ut_ref[:, 2 * k2 * nk:2 * (k2 + 1) * nk])
        for k in (2 * k2, 2 * k2 + 1):
            ak = a2[:, (k - 2 * k2) * nk:(k - 2 * k2 + 1) * nk]
            wn = w_scr[:, n1_0 + k, :, :].reshape(tb, nk)
            act = ak * (1.0 + lax.erf(ak))
            pieces.append(wn * act.astype(BF16))

    @pl.when(i == 0)
    def _route_only():
        for k in range(units):
            route(k)

    @pl.when(i > 0)
    def _route_and_evaluate():
        pieces = []
        per = nchunk // units
        for k in range(units):
            route(k)
            for k2 in range(k * per, (k + 1) * per):
                activate(k2, pieces)
        wg = jnp.concatenate(pieces, axis=1)
        tn = 256
        for j in range(d // tn):
            acc_scr[:, j * tn:(j + 1) * tn] += _dot(wg, v_ref[:, j * tn:(j + 1) * tn])

    @pl.when((i > 0) & (e == pl.num_programs(1) - 1))
    def _fin():
        x2 = x1_ref[...] + g2_ref[0] * acc_scr[...]
        o_ref[...] = _rms(x2) * fg_ref[...]


def _peer_call(st, h2, ut16, v16, x1, mod3, final_g, seq, tb, eb):
    r, d = h2.shape
    ne = v16.shape[0]
    nblk = r // tb
    ngrp = tb // GROUP
    nsteps = ne // eb
    units = PEER_HEADS * ngrp // nsteps
    assert units * nsteps == PEER_HEADS * ngrp and PEER_HEADS % units == 0
    steps_per_grp = PEER_HEADS // units
    slots = PEER_HEADS * PEER_TOPK
    pack = 16
    prev = lambda i: jnp.maximum(i - 1, 0)
    tok = lambda i, e: (prev(i), 0)
    return pl.pallas_call(
        _peer_kernel,
        grid=(nblk + 1, nsteps),
        in_specs=[pl.BlockSpec((2 * units, PEER_NKEYS, GROUP),
                               lambda i, e: (lax.rem(e, steps_per_grp), 0,
                                             jnp.minimum(i, nblk - 1) * ngrp + e // steps_per_grp)),
                  pl.BlockSpec((tb, d), tok),
                  pl.BlockSpec((d, eb), lambda i, e: (0, e)),
                  pl.BlockSpec((eb, d), lambda i, e: (e, 0)),
                  pl.BlockSpec((tb, d), tok),
                  pl.BlockSpec((1, 1, d), lambda i, e: (5 + N_MOD * ((prev(i) * tb) // seq), 0, 0)),
                  pl.BlockSpec((1, d), lambda i, e: (0, 0))],
        out_specs=pl.BlockSpec((tb, d), tok),
        out_shape=jax.ShapeDtypeStruct((r, d), F32),
        scratch_shapes=[pltpu.VMEM((tb // pack, PEER_NKEYS, pack, PEER_NKEYS), BF16),
                        pltpu.VMEM((2 * pack, PEER_NKEYS, PEER_NKEYS), F32),
                        pltpu.VMEM((PEER_NKEYS, pack, PEER_NKEYS), BF16),
                        pltpu.VMEM((PEER_NKEYS, pack, PEER_NKEYS), BF16),
                        pltpu.VMEM((tb, d), F32),
                        pltpu.VMEM((2, 3, ngrp, slots, GROUP), F32),
                        pltpu.VMEM((3, tb, slots), F32)],
        name="peer",
        compiler_params=_params(("arbitrary", "arbitrary")),
    )(st, h2, ut16, v16, x1, mod3, final_g)


def kernel(x, c, ctx, c_ctx, w_mod, b_mod, norm1_g, norm2_g, w_in, pool_w, pool_scale, hg_lb_logits, hg_norm_g,
           w_a, w_b, w_o, peer_wq, peer_keys, peer_u, peer_v, final_g):
    batch, seq, d = x.shape
    ctx_len = ctx.shape[1]
    assert w_mod.shape[0] == 1 and seq == GRID_W * GRID_W and hg_lb_logits.shape[1] == 2
    hw = HG_HEADS * GROUP

    cvec = jnp.concatenate([c, c_ctx[None, :], jnp.zeros((8 - batch - 1, d), F32)], axis=0)
    mod = _mod_call(cvec, w_mod[0], b_mod[0][None, :])
    mod3 = mod.reshape(8 * N_MOD, 1, d)

    w_in16 = w_in[0].astype(BF16)
    f0, f1 = 3 * hw, 5 * hw
    w_in16 = jnp.concatenate([w_in16[:, :f0], w_in16[:, f1:], w_in16[:, f0:f1]], axis=1)
    g1n = norm1_g[0][None, :]
    x2d = x.reshape(batch * seq, d)
    z16, zf = _inproj_call(x2d, mod3, 0, 1, seq, g1n, w_in16, TM_INPROJ, 2 * hw)
    w_ctx16 = jnp.concatenate([w_in16[:, 2 * hw:4 * hw], w_in16[:, -2 * hw:]], axis=1)
    zc16, zcf = _inproj_call(ctx.reshape(batch * ctx_len, d), mod3, N_MOD * batch, N_MOD * batch + 1, None, g1n,
                             w_ctx16, TM_CTX, 2 * hw)

    lbl = hg_lb_logits.reshape(2 * hg_lb_logits.shape[1], hw)
    sf, sb = _hgrn_ctx_call(zc16, zcf, lbl, batch, ctx_len)
    yhg = _hgrn_call(z16, zf, lbl, hg_norm_g[0][None, :], sf, sb, batch, seq)
    ypool = _pool_call(z16, pool_w[0], pool_scale[0][None, :], batch, seq)

    nhp = 2 * PEER_HEADS
    x1, h2, st = _merge_call(x2d, ypool, yhg, z16, mod3, norm2_g[0][None, :],
                             w_a[0].astype(BF16), w_b[0].astype(BF16), w_o[0].astype(BF16),
                             peer_wq[0].T.astype(BF16),
                             peer_keys[0].reshape(nhp, PEER_NKEYS, GROUP).astype(BF16), seq, TM_MERGE)
    out = _peer_call(st, h2, peer_u[0].T.astype(BF16), peer_v[0].astype(BF16), x1, mod3,
                     final_g[None, :], seq, TB_PEER, EB_PEER)
    return out.reshape(batch, seq, d)
```

```python
import functools
import math

import jax
import jax.numpy as jnp
from jax import lax
from jax.experimental import pallas as pl
from jax.experimental.pallas import tpu as pltpu

F32 = jnp.float32
BF16 = jnp.bfloat16
I32 = jnp.int32

EPS = 1e-6
GRID_W = 64
POOL_WINDOWS = (2, 4, 8, 16)
POOL_PAD = 8
GROUP = 128
HG_HEADS = 4
HG_CHUNK = 16
HG_BLOCK = 128
HG_MID = 8
PEER_HEADS = 8
PEER_NKEYS = 128
PEER_TOPK = 16
N_MOD = 6
VMEM_LIMIT = 60 * 1024 * 1024
TM_INPROJ = 2048
TM_CTX = 256
TM_MERGE = 512
TB_PEER = 512
EB_PEER = 2048

_NT = (((1,), (1,)), ((), ()))
_SQRT_HALF = math.sqrt(0.5)


def _dot(a, b):
    return jnp.dot(a, b, preferred_element_type=F32)


def _dot_nt(a, b):
    return lax.dot_general(a, b, _NT, preferred_element_type=F32)


def _hi_lo(a):
    hi = a.astype(BF16)
    return hi, (a - hi.astype(F32)).astype(BF16)


def _sigmoid(x):
    return 1.0 / (1.0 + jnp.exp(-x))


def _rms(x):
    return x * lax.rsqrt(jnp.mean(x * x, axis=-1, keepdims=True) + EPS)


def _params(sem):
    return pltpu.CompilerParams(dimension_semantics=sem, vmem_limit_bytes=VMEM_LIMIT)


def _mod_kernel(c_ref, w_ref, b_ref, o_ref):
    c = c_ref[...]
    s = c * _sigmoid(c)
    s_hi, s_lo = _hi_lo(s)
    w_hi, w_lo = _hi_lo(w_ref[...])
    o_ref[...] = _dot(s_hi, w_hi) + (_dot(s_hi, w_lo) + _dot(s_lo, w_hi)) + b_ref[...]


def _mod_call(cvec, w_mod, b_mod):
    rows, d = cvec.shape
    n = w_mod.shape[1]
    tn = n // 4
    return pl.pallas_call(
        _mod_kernel,
        grid=(n // tn,),
        in_specs=[pl.BlockSpec((rows, d), lambda j: (0, 0)),
                  pl.BlockSpec((d, tn), lambda j: (0, j)),
                  pl.BlockSpec((1, tn), lambda j: (0, j))],
        out_specs=pl.BlockSpec((rows, tn), lambda j: (0, j)),
        out_shape=jax.ShapeDtypeStruct((rows, n), F32),
        name="mod",
        compiler_params=_params(("arbitrary",)),
    )(cvec, w_mod, b_mod)


def _inproj_kernel(x_ref, sh_ref, sc_ref, g_ref, w_ref, o16_ref, of_ref, h_scr):
    j = pl.program_id(1)
    last = pl.num_programs(1) - 1

    @pl.when(j == 0)
    def _():
        h = _rms(x_ref[...]) * g_ref[...] * (1.0 + sc_ref[0]) + sh_ref[0]
        h_scr[...] = h.astype(BF16)

    y = _dot(h_scr[...], w_ref[...])

    @pl.when(j < last)
    def _():
        o16_ref[...] = y.astype(BF16)

    @pl.when(j == last)
    def _():
        of_ref[...] = y


def _inproj_call(x2d, mod3, sh_row, sc_row, rows_per_mod, g, w16, tm, tn):
    r, d = x2d.shape
    ncols = w16.shape[1] // tn
    if rows_per_mod is None:
        sh_map = lambda i, j: (sh_row, 0, 0)
        sc_map = lambda i, j: (sc_row, 0, 0)
    else:
        sh_map = lambda i, j: (sh_row + N_MOD * ((i * tm) // rows_per_mod), 0, 0)
        sc_map = lambda i, j: (sc_row + N_MOD * ((i * tm) // rows_per_mod), 0, 0)
    return pl.pallas_call(
        _inproj_kernel,
        grid=(r // tm, ncols),
        in_specs=[pl.BlockSpec((tm, d), lambda i, j: (i, 0)),
                  pl.BlockSpec((1, 1, d), sh_map),
                  pl.BlockSpec((1, 1, d), sc_map),
                  pl.BlockSpec((1, d), lambda i, j: (0, 0)),
                  pl.BlockSpec((d, tn), lambda i, j: (0, j))],
        out_specs=[pl.BlockSpec((tm, tn), lambda i, j: (i, jnp.minimum(j, ncols - 2))),
                   pl.BlockSpec((tm, tn), lambda i, j: (i, 0))],
        out_shape=[jax.ShapeDtypeStruct((r, (ncols - 1) * tn), BF16),
                   jax.ShapeDtypeStruct((r, tn), F32)],
        scratch_shapes=[pltpu.VMEM((tm, d), BF16)],
        name="inproj",
        compiler_params=_params(("arbitrary", "arbitrary")),
    )(x2d, mod3, mod3, g, w16)


def _pool_kernel(p_ref, w_ref, sc_ref, o_ref, pad_scr, tmp_scr):
    gw, pp = GRID_W, POOL_PAD
    pad_scr[...] = jnp.zeros(pad_scr.shape, F32)
    r_i = lax.broadcasted_iota(I32, (gw, gw, GROUP), 0)
    c_i = lax.broadcasted_iota(I32, (gw, gw, GROUP), 1)
    for g, win in enumerate(POOL_WINDOWS):
        half = win // 2
        pg = p_ref[:, g * GROUP:(g + 1) * GROUP].astype(F32).reshape(gw, gw, GROUP)
        pad_scr[pp:pp + gw, pp:pp + gw, :] = pg
        acc = None
        for k in range(-half, win - half):
            sl = pad_scr[pp + k:pp + k + gw, :, :]
            acc = sl if acc is None else acc + sl
        tmp_scr[...] = acc
        acc = None
        for k in range(-half, win - half):
            sl = tmp_scr[:, pp + k:pp + k + gw, :]
            acc = sl if acc is None else acc + sl
        cr = jnp.minimum(r_i - half + win, gw) - jnp.maximum(r_i - half, 0)
        cc = jnp.minimum(c_i - half + win, gw) - jnp.maximum(c_i - half, 0)
        mean = acc / (cr * cc).astype(F32)
        d = (mean - pg).reshape(gw * gw, GROUP)
        y = _dot(d.astype(BF16), w_ref[g].astype(BF16)) * sc_ref[:, g * GROUP:(g + 1) * GROUP]
        o_ref[:, g * GROUP:(g + 1) * GROUP] = y.astype(BF16)


def _pool_call(z, pool_w, pool_scale, batch, seq):
    width = len(POOL_WINDOWS) * GROUP
    ext = GRID_W + 2 * POOL_PAD
    return pl.pallas_call(
        _pool_kernel,
        grid=(batch,),
        in_specs=[pl.BlockSpec((seq, width), lambda b: (b, 0)),
                  pl.BlockSpec(pool_w.shape, lambda b: (0, 0, 0)),
                  pl.BlockSpec((1, width), lambda b: (0, 0))],
        out_specs=pl.BlockSpec((seq, width), lambda b: (b, 0)),
        out_shape=jax.ShapeDtypeStruct((batch * seq, width), BF16),
        scratch_shapes=[pltpu.VMEM((ext, ext, GROUP), F32), pltpu.VMEM((GRID_W, ext, GROUP), F32)],
        name="pool",
        compiler_params=_params(("arbitrary",)),
    )(z, pool_w, pool_scale)


def _hgrn_masks(rev):
    t, c = HG_BLOCK, HG_CHUNK
    row = lax.broadcasted_iota(I32, (t, t), 0)
    col = lax.broadcasted_iota(I32, (t, t), 1)
    sh = c.bit_length() - 1
    same = (row >> sh) == (col >> sh)
    tri = (col >= row) if rev else (col <= row)
    mid = ((row >> sh) << sh) + HG_MID
    to_mid = (col >= mid) if rev else (col <= mid)
    m_all = jnp.concatenate([jnp.where(same & tri, 1.0, 0.0),
                             jnp.where(same, 1.0, 0.0),
                             jnp.where(same & to_mid, 1.0, 0.0)], axis=0).astype(BF16)
    return m_all, same & tri


def _hgrn_blocks(chains, masks, emit_out):
    t, c = HG_BLOCK, HG_CHUNK
    nch = t // c
    nc = len(chains)
    revs = [ch[0] for ch in chains]
    qs = [ch[2] for ch in chains]
    vs = [ch[3] for ch in chains]
    sts = [ch[5] for ch in chains]
    ks, bs = [], []
    for rev, x_f, _, _, lb, _ in chains:
        f = lb + (1.0 - lb) * _sigmoid(x_f)
        g_hi, g_lo = _hi_lo(jnp.log(f))
        m_all = masks[rev][0]
        ks.append(1.0 - f)
        b2 = _dot(m_all, jnp.concatenate([g_hi, g_lo], axis=1))
        bs.append(b2[:, :GROUP] + b2[:, GROUP:])
    b_loc = [b[0:t] for b in bs]
    b_tot = [b[t:2 * t] for b in bs]
    b_mid = [b[2 * t:3 * t] for b in bs]
    kd16 = [(ks[i] * jnp.exp(b_tot[i] - b_loc[i])).astype(BF16) for i in range(nc)]
    v16 = [v.astype(BF16) for v in vs]
    a_all = [jnp.exp(b) for b in b_tot]
    os_ = [None] * nc
    if emit_out:
        a16 = [(qs[i] * jnp.exp(b_loc[i])).astype(BF16) for i in range(nc)]
        qd16 = [(qs[i] * jnp.exp(b_loc[i] - b_mid[i])).astype(BF16) for i in range(nc)]
        kg16 = [(ks[i] * jnp.exp(b_mid[i] - b_loc[i])).astype(BF16) for i in range(nc)]
        attn = [jnp.where(masks[revs[i]][1], _dot_nt(qd16[i], kg16[i]), 0.0) for i in range(nc)]
        os_ = [_dot(attn[i].astype(BF16), v16[i]) for i in range(nc)]
        o_inter = [[None] * nch for _ in range(nc)]
    for step in range(nch):
        for i in range(nc):
            n = nch - 1 - step if revs[i] else step
            if emit_out:
                o_inter[i][n] = _dot_nt(a16[i][n * c:(n + 1) * c], sts[i].astype(BF16))
            ut = lax.dot_general(v16[i][n * c:(n + 1) * c], kd16[i][n * c:(n + 1) * c], (((0,), (0,)), ((), ())),
                                 preferred_element_type=F32)
            sts[i] = sts[i] * a_all[i][n * c:n * c + 1, :] + ut
    if emit_out:
        os_ = [os_[i] + jnp.concatenate(o_inter[i], axis=0) for i in range(nc)]
    return list(zip(os_, sts))


def _lower_bounds(lbl_ref):
    out = []
    for d in range(2):
        l0, l1 = lbl_ref[2 * d:2 * d + 1, :], lbl_ref[2 * d + 1:2 * d + 2, :]
        m = jnp.maximum(l0, l1)
        e0, e1 = jnp.exp(l0 - m), jnp.exp(l1 - m)
        out.append(e0 / (e0 + e1))
    return out


def _hgrn_run(nheads, seq, lbl_ref, f_refs, q_ref, v_ref, st_scr, o_scr):
    t = HG_BLOCK
    nblk = seq // t
    masks = (_hgrn_masks(False), _hgrn_masks(True))
    lbs = _lower_bounds(lbl_ref)
    emit_out = o_scr is not None

    def body(i, carry, accumulate):
        chains = []
        for d in range(2):
            blk = (nblk - 1 - i) if d else i
            rows = pl.ds(pl.multiple_of(blk * t, t), t)
            for h in range(nheads):
                ls = slice(h * GROUP, (h + 1) * GROUP)
                q = q_ref[rows, ls].astype(F32) if emit_out else None
                chains.append((d, h, rows, ls, f_refs[d][rows, ls], q, v_ref[rows, ls].astype(F32),
                               st_scr[d * nheads + h]))
        results = _hgrn_blocks([(d, x_f, q, v, lbs[d][:, ls], st) for d, h, rows, ls, x_f, q, v, st in chains],
                               masks, emit_out)
        for (d, h, rows, ls, *_), (o, st) in zip(chains, results):
            st_scr[d * nheads + h] = st
            if emit_out and accumulate:
                o_scr[rows, ls] += o
            elif emit_out:
                o_scr[rows, ls] = o
        return carry

    if emit_out:
        assert nblk % 2 == 0
        lax.fori_loop(0, nblk // 2, functools.partial(body, accumulate=False), 0)
        lax.fori_loop(nblk // 2, nblk, functools.partial(body, accumulate=True), 0)
    else:
        lax.fori_loop(0, nblk, functools.partial(body, accumulate=False), 0)


def _hgrn_ctx_kernel(v_ref, ff_ref, fb_ref, lbl_ref, sf_ref, sb_ref, st_scr, *, seq):
    nheads = sf_ref.shape[1]
    st_scr[...] = jnp.zeros(st_scr.shape, F32)
    _hgrn_run(nheads, seq, lbl_ref, (ff_ref, fb_ref), None, v_ref, st_scr, None)
    sf_ref[0] = st_scr[0:nheads]
    sb_ref[0] = st_scr[nheads:2 * nheads]


def _hgrn_kernel(q_ref, v_ref, ff_ref, fb_ref, og_ref, lbl_ref, ng_ref, sf_ref, sb_ref, y_ref,
                 st_scr, o_scr, *, seq):
    nheads = sf_ref.shape[1]
    st_scr[0:nheads] = sf_ref[0]
    st_scr[nheads:2 * nheads] = sb_ref[0]
    _hgrn_run(nheads, seq, lbl_ref, (ff_ref, fb_ref), q_ref, v_ref, st_scr, o_scr)
    rb = 4 * HG_BLOCK

    def epilogue(i, carry):
        rows = pl.ds(pl.multiple_of(i * rb, rb), rb)
        for h in range(nheads):
            ls = slice(h * GROUP, (h + 1) * GROUP)
            og = og_ref[rows, ls].astype(F32)
            y = _rms(o_scr[rows, ls]) * ng_ref[...] * (og * _sigmoid(og))
            y_ref[rows, ls] = y.astype(BF16)
        return carry

    lax.fori_loop(0, seq // rb, epilogue, 0)


def _hgrn_specs(seq):
    hw = HG_HEADS * GROUP
    big = lambda k: pl.BlockSpec((seq, hw), lambda b: (b, k), pipeline_mode=pl.Buffered(1))
    st_spec = pl.BlockSpec((1, HG_HEADS, GROUP, GROUP), lambda b: (b, 0, 0, 0))
    return hw, big, st_spec


def _hgrn_ctx_call(zc16, zcf, lbl, batch, seq):
    hw, big, st_spec = _hgrn_specs(seq)
    st_shape = jax.ShapeDtypeStruct((batch, HG_HEADS, GROUP, GROUP), F32)
    return pl.pallas_call(
        functools.partial(_hgrn_ctx_kernel, seq=seq),
        grid=(batch,),
        in_specs=[big(0), big(0), big(1), pl.BlockSpec((4, hw), lambda b: (0, 0))],
        out_specs=[st_spec, st_spec],
        out_shape=[st_shape, st_shape],
        scratch_shapes=[pltpu.VMEM((2 * HG_HEADS, GROUP, GROUP), F32)],
        name="hgrn_ctx",
        compiler_params=_params(("arbitrary",)),
    )(zc16, zcf, zcf, lbl)


def _hgrn_call(z16, zf, lbl, norm_g, sf, sb, batch, seq):
    hw, big, st_spec = _hgrn_specs(seq)
    return pl.pallas_call(
        functools.partial(_hgrn_kernel, seq=seq),
        grid=(batch,),
        in_specs=[big(1), big(2), big(0), big(1), big(3),
                  pl.BlockSpec((4, hw), lambda b: (0, 0)),
                  pl.BlockSpec((1, GROUP), lambda b: (0, 0)),
                  st_spec, st_spec],
        out_specs=pl.BlockSpec((seq, hw), lambda b: (b, 0)),
        out_shape=jax.ShapeDtypeStruct((batch * seq, hw), BF16),
        scratch_shapes=[pltpu.VMEM((2 * HG_HEADS, GROUP, GROUP), F32), pltpu.VMEM((seq, hw), F32)],
        name="hgrn",
        compiler_params=_params(("arbitrary",)),
    )(z16, z16, zf, zf, z16, lbl, norm_g, sf, sb)


def _merge_kernel(x_ref, yp_ref, yh_ref, gp_ref, gh_ref, g1_ref, sh_ref, sc_ref, ng_ref,
                  wa_ref, wb_ref, wo_ref, wqt_ref, keys_ref, x1_ref, h2_ref, st_ref):
    m = (_sigmoid(gp_ref[...].astype(F32)) * _dot(yp_ref[...], wa_ref[...])
         + _sigmoid(gh_ref[...].astype(F32)) * _dot(yh_ref[...], wb_ref[...]))
    x1 = x_ref[...] + g1_ref[0] * _dot(m.astype(BF16), wo_ref[...])
    x1_ref[...] = x1
    h2 = _rms(x1) * ng_ref[...] * (1.0 + sc_ref[0]) + sh_ref[0]
    h2_ref[...] = (h2 * _SQRT_HALF).astype(BF16)
    qt = _dot(wqt_ref[...], h2.T.astype(BF16))
    for hp in range(2 * PEER_HEADS):
        st_ref[hp] = _dot(keys_ref[hp], qt[hp * GROUP:(hp + 1) * GROUP].astype(BF16))


def _merge_call(x2d, ypool, yhg, z, mod3, norm2_g, wa16, wb16, wo16, wqt16, keys16, seq, tm):
    r, d = x2d.shape
    gcol0 = (z.shape[1] - 2 * d) // d
    modrow = lambda k: (lambda i: (k + N_MOD * ((i * tm) // seq), 0, 0))
    const2 = lambda i: (0, 0)
    nhp = 2 * PEER_HEADS
    return pl.pallas_call(
        _merge_kernel,
        grid=(r // tm,),
        in_specs=[pl.BlockSpec((tm, d), lambda i: (i, 0)),
                  pl.BlockSpec((tm, ypool.shape[1]), lambda i: (i, 0)),
                  pl.BlockSpec((tm, yhg.shape[1]), lambda i: (i, 0)),
                  pl.BlockSpec((tm, d), lambda i: (i, gcol0)),
                  pl.BlockSpec((tm, d), lambda i: (i, gcol0 + 1)),
                  pl.BlockSpec((1, 1, d), modrow(2)),
                  pl.BlockSpec((1, 1, d), modrow(3)),
                  pl.BlockSpec((1, 1, d), modrow(4)),
                  pl.BlockSpec((1, d), const2),
                  pl.BlockSpec(wa16.shape, const2),
                  pl.BlockSpec(wb16.shape, const2),
                  pl.BlockSpec(wo16.shape, const2),
                  pl.BlockSpec(wqt16.shape, const2),
                  pl.BlockSpec(keys16.shape, lambda i: (0, 0, 0))],
        out_specs=[pl.BlockSpec((tm, d), lambda i: (i, 0)),
                   pl.BlockSpec((tm, d), lambda i: (i, 0)),
                   pl.BlockSpec((nhp, PEER_NKEYS, tm), lambda i: (0, 0, i))],
        out_shape=[jax.ShapeDtypeStruct((r, d), F32),
                   jax.ShapeDtypeStruct((r, d), BF16),
                   jax.ShapeDtypeStruct((nhp, PEER_NKEYS, r), F32)],
        name="merge",
        compiler_params=_params(("arbitrary",)),
    )(x2d, ypool, yhg, z, z, mod3, mod3, mod3, norm2_g, wa16, wb16, wo16, wqt16, keys16)


def _top16(v, iota_n):
    vals, idxs = [], []
    for _ in range(PEER_TOPK):
        m = jnp.max(v, axis=0, keepdims=True)
        idx = jnp.min(jnp.where(v == m, iota_n, float(PEER_NKEYS)), axis=0, keepdims=True)
        v = jnp.where(iota_n == idx, -jnp.inf, v)
        vals.append(m)
        idxs.append(idx)
    return jnp.concatenate(vals, axis=0), jnp.concatenate(idxs, axis=0)


def _topk_unit(s1, s2):
    tg = s1.shape[1]
    kk = PEER_TOPK
    nkf = float(PEER_NKEYS)
    iota_n = lax.broadcasted_iota(I32, (PEER_NKEYS, tg), 0).astype(F32)
    v1, n1 = _top16(s1, iota_n)
    v2, n2 = _top16(s2, iota_n)

    r = lax.broadcasted_iota(I32, (8, tg), 0)
    rf = r.astype(F32)
    sel3 = lambda x: jnp.where(r < 5, x[0], x[1])
    sel4 = lambda x: jnp.where(r < 4, x[0], jnp.where(r < 6, x[1], x[2]))
    sel5 = lambda x: jnp.where(r < 2, x[0], x[1])

    def tiles(l1, l2):
        row1 = lambda a: l1[a:a + 1]
        lo2, hi1 = l2[0:8], l1[8:16]
        hi1_r2 = pltpu.roll(hi1, 2, 0)
        t1 = [row1(0), row1(0), row1(1), sel3((row1(2), row1(4))), sel4((row1(3), row1(5), row1(6))),
              sel5((row1(7), hi1_r2)), hi1_r2]
        t2 = [lo2, l2[8:16], lo2, sel3((lo2, pltpu.roll(lo2, 5, 0))),
              sel4((lo2, pltpu.roll(lo2, 4, 0), pltpu.roll(lo2, 6, 0))), sel5((lo2, l2[0:1])), l2[0:1]]
        return t1, t2

    tv1, tv2 = tiles(v1, v2)
    tn1, tn2 = tiles(n1, n2)
    cand = [a + b for a, b in zip(tv1, tv2)]
    cand[6] = jnp.where(r < 2, cand[6], -jnp.inf)
    cand = jnp.concatenate(cand, axis=0)
    eid = jnp.concatenate([a * nkf + b for a, b in zip(tn1, tn2)], axis=0)
    pos = jnp.concatenate([rf, rf + 8.0, rf + 16.0, sel3((rf + 32.0, rf + 59.0)),
                           sel4((rf + 48.0, rf + 76.0, rf + 90.0)), sel5((rf + 112.0, rf * 16.0 + 96.0)),
                           rf * 16.0 + 224.0], axis=0)
    tops, es = [], []
    for _ in range(kk):
        m = jnp.max(cand, axis=0, keepdims=True)
        p = jnp.min(jnp.where(cand == m, pos, 1e9), axis=0, keepdims=True)
        sel = pos == p
        es.append(jnp.max(jnp.where(sel, eid, -1.0), axis=0, keepdims=True))
        cand = jnp.where(sel, -jnp.inf, cand)
        tops.append(m)
    top = jnp.concatenate(tops, axis=0)
    e = jnp.concatenate(es, axis=0)
    ex = jnp.exp(top - top[0:1])
    gate = ex / jnp.sum(ex, axis=0, keepdims=True)
    e1 = jnp.floor(e * (1.0 / PEER_NKEYS))
    return e1, e - e1 * nkf, gate * _SQRT_HALF


def _peer_kernel(s_ref, h2_ref, ut_ref, v_ref, x1_ref, g2_ref, fg_ref, o_ref,
                 w_scr, stage_scr, tr0_scr, tr1_scr, tr2_scr, tr3_scr, acc_scr, rt_scr, rr_scr):
    tb, d = h2_ref.shape
    eb = v_ref.shape[0]
    i = pl.program_id(0)
    e = pl.program_id(1)
    nk = PEER_NKEYS
    ngrp = tb // GROUP
    pack = w_scr.shape[2]
    unroll = stage_scr.shape[0]
    tr_scrs = (tr0_scr, tr1_scr, tr2_scr, tr3_scr)
    units = s_ref.shape[0] // 2
    steps_per_grp = PEER_HEADS // units

    @pl.when((i > 0) & (e == 0))
    def _build():
        rd = lax.rem(i + 1, 2)
        for fld in range(3):
            for g in range(ngrp):
                rr_scr[fld, g * GROUP:(g + 1) * GROUP, :] = rt_scr[rd, fld, g].T
        iota_s = lax.broadcasted_iota(I32, (nk, nk), 0).astype(F32).astype(BF16)
        zero = jnp.zeros((), BF16)
        one = jnp.ones((), BF16)

        def flush(tt):
            for g in range(unroll // pack):
                blk = stage_scr[g * pack:(g + 1) * pack].astype(BF16)
                tr_scrs[g][...] = jnp.transpose(blk, (1, 0, 2))
            for g in range(unroll // pack):
                w_scr[tt * (unroll // pack) + g] = tr_scrs[g][...]

        def body(tt, carry):
            flush(jnp.maximum(tt - 1, 0))
            ts = [tt * unroll + r for r in range(unroll)]
            row = lambda fld, t: jnp.broadcast_to(rr_scr[fld, pl.ds(t, 1), :], (pack, nk)).astype(BF16)
            tiles = [iota_s[v * pack:(v + 1) * pack] for v in range(nk // pack)]
            onehot = lambda idx, val: jnp.concatenate([jnp.where(tl == idx, val, zero) for tl in tiles], axis=0)
            g1t = [onehot(row(0, t), row(2, t)) for t in ts]
            p2t = [onehot(row(1, t), one) for t in ts]
            for r, (g, p) in enumerate(zip(g1t, p2t)):
                stage_scr[r] = _dot_nt(g, p)
            return carry

        stage_scr[...] = jnp.zeros(stage_scr.shape, F32)
        ntrip = tb // unroll
        lax.fori_loop(0, ntrip, body, 0)
        flush(ntrip - 1)
        acc_scr[...] = jnp.zeros(acc_scr.shape, F32)

    grp = e // steps_per_grp
    head0 = lax.rem(e, steps_per_grp) * units
    wr = lax.rem(i, 2)
    n1_0 = e * (eb // nk)
    nchunk = eb // (2 * nk)

    def route(k):
        i1, i2, gate = _topk_unit(s_ref[2 * k], s_ref[2 * k + 1])
        rows = pl.ds(pl.multiple_of((head0 + k) * PEER_TOPK, PEER_TOPK), PEER_TOPK)
        rt_scr[wr, 0, grp, rows, :] = i1
        rt_scr[wr, 1, grp, rows, :] = i2
        rt_scr[wr, 2, grp, rows, :] = gate

    def activate(k2, pieces):
        a2 = _dot(h2_ref[...], ut_ref[:, 2 * k2 * nk:2 * (k2 + 1) * nk])
        for k in (2 * k2, 2 * k2 + 1):
            ak = a2[:, (k - 2 * k2) * nk:(k - 2 * k2 + 1) * nk]
            wn = w_scr[:, n1_0 + k, :, :].reshape(tb, nk)
            act = ak * (1.0 + lax.erf(ak))
            pieces.append(wn * act.astype(BF16))

    @pl.when(i == 0)
    def _route_only():
        for k in range(units):
            route(k)

    @pl.when(i > 0)
    def _route_and_evaluate():
        pieces = []
        per = nchunk // units
        for k in range(units):
            route(k)
            for k2 in range(k * per, (k + 1) * per):
                activate(k2, pieces)
        wg = jnp.concatenate(pieces, axis=1)
        tn = 256
        for j in range(d // tn):
            acc_scr[:, j * tn:(j + 1) * tn] += _dot(wg, v_ref[:, j * tn:(j + 1) * tn])

    @pl.when((i > 0) & (e == pl.num_programs(1) - 1))
    def _fin():
        x2 = x1_ref[...] + g2_ref[0] * acc_scr[...]
        o_ref[...] = _rms(x2) * fg_ref[...]


def _peer_call(st, h2, ut16, v16, x1, mod3, final_g, seq, tb, eb):
    r, d = h2.shape
    ne = v16.shape[0]
    nblk = r // tb
    ngrp = tb // GROUP
    nsteps = ne // eb
    units = PEER_HEADS * ngrp // nsteps
    assert units * nsteps == PEER_HEADS * ngrp and PEER_HEADS % units == 0
    steps_per_grp = PEER_HEADS // units
    slots = PEER_HEADS * PEER_TOPK
    pack = 16
    prev = lambda i: jnp.maximum(i - 1, 0)
    tok = lambda i, e: (prev(i), 0)
    return pl.pallas_call(
        _peer_kernel,
        grid=(nblk + 1, nsteps),
        in_specs=[pl.BlockSpec((2 * units, PEER_NKEYS, GROUP),
                               lambda i, e: (lax.rem(e, steps_per_grp), 0,
                                             jnp.minimum(i, nblk - 1) * ngrp + e // steps_per_grp)),
                  pl.BlockSpec((tb, d), tok),
                  pl.BlockSpec((d, eb), lambda i, e: (0, e)),
                  pl.BlockSpec((eb, d), lambda i, e: (e, 0)),
                  pl.BlockSpec((tb, d), tok),
                  pl.BlockSpec((1, 1, d), lambda i, e: (5 + N_MOD * ((prev(i) * tb) // seq), 0, 0)),
                  pl.BlockSpec((1, d), lambda i, e: (0, 0))],
        out_specs=pl.BlockSpec((tb, d), tok),
        out_shape=jax.ShapeDtypeStruct((r, d), F32),
        scratch_shapes=[pltpu.VMEM((tb // pack, PEER_NKEYS, pack, PEER_NKEYS), BF16),
                        pltpu.VMEM((4 * pack, PEER_NKEYS, PEER_NKEYS), F32),
                        pltpu.VMEM((PEER_NKEYS, pack, PEER_NKEYS), BF16),
                        pltpu.VMEM((PEER_NKEYS, pack, PEER_NKEYS), BF16),
                        pltpu.VMEM((PEER_NKEYS, pack, PEER_NKEYS), BF16),
                        pltpu.VMEM((PEER_NKEYS, pack, PEER_NKEYS), BF16),
                        pltpu.VMEM((tb, d), F32),
                        pltpu.VMEM((2, 3, ngrp, slots, GROUP), F32),
                        pltpu.VMEM((3, tb, slots), F32)],
        name="peer",
        compiler_params=_params(("arbitrary", "arbitrary")),
    )(st, h2, ut16, v16, x1, mod3, final_g)


def kernel(x, c, ctx, c_ctx, w_mod, b_mod, norm1_g, norm2_g, w_in, pool_w, pool_scale, hg_lb_logits, hg_norm_g,
           w_a, w_b, w_o, peer_wq, peer_keys, peer_u, peer_v, final_g):
    batch, seq, d = x.shape
    ctx_len = ctx.shape[1]
    assert w_mod.shape[0] == 1 and seq == GRID_W * GRID_W and hg_lb_logits.shape[1] == 2
    hw = HG_HEADS * GROUP

    cvec = jnp.concatenate([c, c_ctx[None, :], jnp.zeros((8 - batch - 1, d), F32)], axis=0)
    mod = _mod_call(cvec, w_mod[0], b_mod[0][None, :])
    mod3 = mod.reshape(8 * N_MOD, 1, d)

    w_in16 = w_in[0].astype(BF16)
    f0, f1 = 3 * hw, 5 * hw
    w_in16 = jnp.concatenate([w_in16[:, :f0], w_in16[:, f1:], w_in16[:, f0:f1]], axis=1)
    g1n = norm1_g[0][None, :]
    x2d = x.reshape(batch * seq, d)
    z16, zf = _inproj_call(x2d, mod3, 0, 1, seq, g1n, w_in16, TM_INPROJ, 2 * hw)
    w_ctx16 = jnp.concatenate([w_in16[:, 2 * hw:4 * hw], w_in16[:, -2 * hw:]], axis=1)
    zc16, zcf = _inproj_call(ctx.reshape(batch * ctx_len, d), mod3, N_MOD * batch, N_MOD * batch + 1, None, g1n,
                             w_ctx16, TM_CTX, 2 * hw)

    lbl = hg_lb_logits.reshape(2 * hg_lb_logits.shape[1], hw)
    sf, sb = _hgrn_ctx_call(zc16, zcf, lbl, batch, ctx_len)
    yhg = _hgrn_call(z16, zf, lbl, hg_norm_g[0][None, :], sf, sb, batch, seq)
    ypool = _pool_call(z16, pool_w[0], pool_scale[0][None, :], batch, seq)

    nhp = 2 * PEER_HEADS
    x1, h2, st = _merge_call(x2d, ypool, yhg, z16, mod3, norm2_g[0][None, :],
                             w_a[0].astype(BF16), w_b[0].astype(BF16), w_o[0].astype(BF16),
                             peer_wq[0].T.astype(BF16),
                             peer_keys[0].reshape(nhp, PEER_NKEYS, GROUP).astype(BF16), seq, TM_MERGE)
    out = _peer_call(st, h2, peer_u[0].T.astype(BF16), peer_v[0].astype(BF16), x1, mod3,
                     final_g[None, :], seq, TB_PEER, EB_PEER)
    return out.reshape(batch, seq, d)
```

```python
import functools
import math

import jax
import jax.numpy as jnp
from jax import lax
from jax.experimental import pallas as pl
from jax.experimental.pallas import tpu as pltpu

F32 = jnp.float32
BF16 = jnp.bfloat16
I32 = jnp.int32

EPS = 1e-6
GRID_W = 64
POOL_WINDOWS = (2, 4, 8, 16)
POOL_PAD = 8
GROUP = 128
HG_HEADS = 4
HG_CHUNK = 16
HG_BLOCK = 128
HG_MID = 8
PEER_HEADS = 8
PEER_NKEYS = 128
PEER_TOPK = 16
N_MOD = 6
VMEM_LIMIT = 60 * 1024 * 1024
TM_INPROJ = 2048
TM_CTX = 256
TM_MERGE = 512
TB_PEER = 512
EB_PEER = 2048

_NT = (((1,), (1,)), ((), ()))
_SQRT_HALF = math.sqrt(0.5)


def _dot(a, b):
    return jnp.dot(a, b, preferred_element_type=F32)


def _dot_nt(a, b):
    return lax.dot_general(a, b, _NT, preferred_element_type=F32)


def _hi_lo(a):
    hi = a.astype(BF16)
    return hi, (a - hi.astype(F32)).astype(BF16)


def _sigmoid(x):
    return 1.0 / (1.0 + jnp.exp(-x))


def _rms(x):
    return x * lax.rsqrt(jnp.mean(x * x, axis=-1, keepdims=True) + EPS)


def _params(sem):
    return pltpu.CompilerParams(dimension_semantics=sem, vmem_limit_bytes=VMEM_LIMIT)


def _mod_kernel(c_ref, w_ref, b_ref, o_ref):
    c = c_ref[...]
    s = c * _sigmoid(c)
    s_hi, s_lo = _hi_lo(s)
    w_hi, w_lo = _hi_lo(w_ref[...])
    o_ref[...] = _dot(s_hi, w_hi) + (_dot(s_hi, w_lo) + _dot(s_lo, w_hi)) + b_ref[...]


def _mod_call(cvec, w_mod, b_mod):
    rows, d = cvec.shape
    n = w_mod.shape[1]
    tn = n // 4
    return pl.pallas_call(
        _mod_kernel,
        grid=(n // tn,),
        in_specs=[pl.BlockSpec((rows, d), lambda j: (0, 0)),
                  pl.BlockSpec((d, tn), lambda j: (0, j)),
                  pl.BlockSpec((1, tn), lambda j: (0, j))],
        out_specs=pl.BlockSpec((rows, tn), lambda j: (0, j)),
        out_shape=jax.ShapeDtypeStruct((rows, n), F32),
        name="mod",
        compiler_params=_params(("arbitrary",)),
    )(cvec, w_mod, b_mod)


def _inproj_kernel(x_ref, sh_ref, sc_ref, g_ref, w_ref, o16_ref, of_ref, h_scr):
    j = pl.program_id(1)
    last = pl.num_programs(1) - 1

    @pl.when(j == 0)
    def _():
        h = _rms(x_ref[...]) * g_ref[...] * (1.0 + sc_ref[0]) + sh_ref[0]
        h_scr[...] = h.astype(BF16)

    y = _dot(h_scr[...], w_ref[...])

    @pl.when(j < last)
    def _():
        o16_ref[...] = y.astype(BF16)

    @pl.when(j == last)
    def _():
        of_ref[...] = y


def _inproj_call(x2d, mod3, sh_row, sc_row, rows_per_mod, g, w16, tm, tn):
    r, d = x2d.shape
    ncols = w16.shape[1] // tn
    if rows_per_mod is None:
        sh_map = lambda i, j: (sh_row, 0, 0)
        sc_map = lambda i, j: (sc_row, 0, 0)
    else:
        sh_map = lambda i, j: (sh_row + N_MOD * ((i * tm) // rows_per_mod), 0, 0)
        sc_map = lambda i, j: (sc_row + N_MOD * ((i * tm) // rows_per_mod), 0, 0)
    return pl.pallas_call(
        _inproj_kernel,
        grid=(r // tm, ncols),
        in_specs=[pl.BlockSpec((tm, d), lambda i, j: (i, 0)),
                  pl.BlockSpec((1, 1, d), sh_map),
                  pl.BlockSpec((1, 1, d), sc_map),
                  pl.BlockSpec((1, d), lambda i, j: (0, 0)),
                  pl.BlockSpec((d, tn), lambda i, j: (0, j))],
        out_specs=[pl.BlockSpec((tm, tn), lambda i, j: (i, jnp.minimum(j, ncols - 2))),
                   pl.BlockSpec((tm, tn), lambda i, j: (i, 0))],
        out_shape=[jax.ShapeDtypeStruct((r, (ncols - 1) * tn), BF16),
                   jax.ShapeDtypeStruct((r, tn), F32)],
        scratch_shapes=[pltpu.VMEM((tm, d), BF16)],
        name="inproj",
        compiler_params=_params(("arbitrary", "arbitrary")),
    )(x2d, mod3, mod3, g, w16)


def _pool_kernel(p_ref, w_ref, sc_ref, o_ref, pad_scr, tmp_scr):
    gw, pp = GRID_W, POOL_PAD
    pad_scr[...] = jnp.zeros(pad_scr.shape, F32)
    r_i = lax.broadcasted_iota(I32, (gw, gw, GROUP), 0)
    c_i = lax.broadcasted_iota(I32, (gw, gw, GROUP), 1)
    for g, win in enumerate(POOL_WINDOWS):
        half = win // 2
        pg = p_ref[:, g * GROUP:(g + 1) * GROUP].astype(F32).reshape(gw, gw, GROUP)
        pad_scr[pp:pp + gw, pp:pp + gw, :] = pg
        acc = None
        for k in range(-half, win - half):
            sl = pad_scr[pp + k:pp + k + gw, :, :]
            acc = sl if acc is None else acc + sl
        tmp_scr[...] = acc
        acc = None
        for k in range(-half, win - half):
            sl = tmp_scr[:, pp + k:pp + k + gw, :]
            acc = sl if acc is None else acc + sl
        cr = jnp.minimum(r_i - half + win, gw) - jnp.maximum(r_i - half, 0)
        cc = jnp.minimum(c_i - half + win, gw) - jnp.maximum(c_i - half, 0)
        mean = acc / (cr * cc).astype(F32)
        d = (mean - pg).reshape(gw * gw, GROUP)
        y = _dot(d.astype(BF16), w_ref[g].astype(BF16)) * sc_ref[:, g * GROUP:(g + 1) * GROUP]
        o_ref[:, g * GROUP:(g + 1) * GROUP] = y.astype(BF16)


def _pool_call(z, pool_w, pool_scale, batch, seq):
    width = len(POOL_WINDOWS) * GROUP
    ext = GRID_W + 2 * POOL_PAD
    return pl.pallas_call(
        _pool_kernel,
        grid=(batch,),
        in_specs=[pl.BlockSpec((seq, width), lambda b: (b, 0)),
                  pl.BlockSpec(pool_w.shape, lambda b: (0, 0, 0)),
                  pl.BlockSpec((1, width), lambda b: (0, 0))],
        out_specs=pl.BlockSpec((seq, width), lambda b: (b, 0)),
        out_shape=jax.ShapeDtypeStruct((batch * seq, width), BF16),
        scratch_shapes=[pltpu.VMEM((ext, ext, GROUP), F32), pltpu.VMEM((GRID_W, ext, GROUP), F32)],
        name="pool",
        compiler_params=_params(("arbitrary",)),
    )(z, pool_w, pool_scale)


def _hgrn_masks(rev):
    t, c = HG_BLOCK, HG_CHUNK
    row = lax.broadcasted_iota(I32, (t, t), 0)
    col = lax.broadcasted_iota(I32, (t, t), 1)
    sh = c.bit_length() - 1
    same = (row >> sh) == (col >> sh)
    tri = (col >= row) if rev else (col <= row)
    mid = ((row >> sh) << sh) + HG_MID
    to_mid = (col >= mid) if rev else (col <= mid)
    m_all = jnp.concatenate([jnp.where(same & tri, 1.0, 0.0),
                             jnp.where(same, 1.0, 0.0),
                             jnp.where(same & to_mid, 1.0, 0.0)], axis=0).astype(BF16)
    return m_all, same & tri


def _hgrn_blocks(chains, masks, emit_out):
    t, c = HG_BLOCK, HG_CHUNK
    nch = t // c
    nc = len(chains)
    revs = [ch[0] for ch in chains]
    qs = [ch[2] for ch in chains]
    vs = [ch[3] for ch in chains]
    sts = [ch[5] for ch in chains]
    ks, bs = [], []
    for rev, x_f, _, _, lb, _ in chains:
        f = lb + (1.0 - lb) * _sigmoid(x_f)
        g_hi, g_lo = _hi_lo(jnp.log(f))
        m_all = masks[rev][0]
        ks.append(1.0 - f)
        b2 = _dot(m_all, jnp.concatenate([g_hi, g_lo], axis=1))
        bs.append(b2[:, :GROUP] + b2[:, GROUP:])
    b_loc = [b[0:t] for b in bs]
    b_tot = [b[t:2 * t] for b in bs]
    b_mid = [b[2 * t:3 * t] for b in bs]
    kd16 = [(ks[i] * jnp.exp(b_tot[i] - b_loc[i])).astype(BF16) for i in range(nc)]
    v16 = [v.astype(BF16) for v in vs]
    a_all = [jnp.exp(b) for b in b_tot]
    os_ = [None] * nc
    if emit_out:
        a16 = [(qs[i] * jnp.exp(b_loc[i])).astype(BF16) for i in range(nc)]
        qd16 = [(qs[i] * jnp.exp(b_loc[i] - b_mid[i])).astype(BF16) for i in range(nc)]
        kg16 = [(ks[i] * jnp.exp(b_mid[i] - b_loc[i])).astype(BF16) for i in range(nc)]
        attn = [jnp.where(masks[revs[i]][1], _dot_nt(qd16[i], kg16[i]), 0.0) for i in range(nc)]
        os_ = [_dot(attn[i].astype(BF16), v16[i]) for i in range(nc)]
        o_inter = [[None] * nch for _ in range(nc)]
    for step in range(nch):
        for i in range(nc):
            n = nch - 1 - step if revs[i] else step
            if emit_out:
                o_inter[i][n] = _dot_nt(a16[i][n * c:(n + 1) * c], sts[i].astype(BF16))
            ut = lax.dot_general(v16[i][n * c:(n + 1) * c], kd16[i][n * c:(n + 1) * c], (((0,), (0,)), ((), ())),
                                 preferred_element_type=F32)
            sts[i] = sts[i] * a_all[i][n * c:n * c + 1, :] + ut
    if emit_out:
        os_ = [os_[i] + jnp.concatenate(o_inter[i], axis=0) for i in range(nc)]
    return list(zip(os_, sts))


def _lower_bounds(lbl_ref):
    out = []
    for d in range(2):
        l0, l1 = lbl_ref[2 * d:2 * d + 1, :], lbl_ref[2 * d + 1:2 * d + 2, :]
        m = jnp.maximum(l0, l1)
        e0, e1 = jnp.exp(l0 - m), jnp.exp(l1 - m)
        out.append(e0 / (e0 + e1))
    return out


def _hgrn_run(nheads, seq, lbl_ref, f_refs, q_ref, v_ref, st_scr, o_scr):
    t = HG_BLOCK
    nblk = seq // t
    masks = (_hgrn_masks(False), _hgrn_masks(True))
    lbs = _lower_bounds(lbl_ref)
    emit_out = o_scr is not None

    def body(i, carry, accumulate):
        chains = []
        for d in range(2):
            blk = (nblk - 1 - i) if d else i
            rows = pl.ds(pl.multiple_of(blk * t, t), t)
            for h in range(nheads):
                ls = slice(h * GROUP, (h + 1) * GROUP)
                q = q_ref[rows, ls].astype(F32) if emit_out else None
                chains.append((d, h, rows, ls, f_refs[d][rows, ls], q, v_ref[rows, ls].astype(F32),
                               st_scr[d * nheads + h]))
        results = _hgrn_blocks([(d, x_f, q, v, lbs[d][:, ls], st) for d, h, rows, ls, x_f, q, v, st in chains],
                               masks, emit_out)
        for (d, h, rows, ls, *_), (o, st) in zip(chains, results):
            st_scr[d * nheads + h] = st
            if emit_out and accumulate:
                o_scr[rows, ls] += o
            elif emit_out:
                o_scr[rows, ls] = o
        return carry

    if emit_out:
        assert nblk % 2 == 0
        lax.fori_loop(0, nblk // 2, functools.partial(body, accumulate=False), 0)
        lax.fori_loop(nblk // 2, nblk, functools.partial(body, accumulate=True), 0)
    else:
        lax.fori_loop(0, nblk, functools.partial(body, accumulate=False), 0)


def _hgrn_ctx_kernel(v_ref, ff_ref, fb_ref, lbl_ref, sf_ref, sb_ref, st_scr, *, seq):
    nheads = sf_ref.shape[1]
    st_scr[...] = jnp.zeros(st_scr.shape, F32)
    _hgrn_run(nheads, seq, lbl_ref, (ff_ref, fb_ref), None, v_ref, st_scr, None)
    sf_ref[0] = st_scr[0:nheads]
    sb_ref[0] = st_scr[nheads:2 * nheads]


def _hgrn_kernel(q_ref, v_ref, ff_ref, fb_ref, og_ref, lbl_ref, ng_ref, sf_ref, sb_ref, y_ref,
                 st_scr, o_scr, *, seq):
    nheads = sf_ref.shape[1]
    st_scr[0:nheads] = sf_ref[0]
    st_scr[nheads:2 * nheads] = sb_ref[0]
    _hgrn_run(nheads, seq, lbl_ref, (ff_ref, fb_ref), q_ref, v_ref, st_scr, o_scr)
    rb = 4 * HG_BLOCK

    def epilogue(i, carry):
        rows = pl.ds(pl.multiple_of(i * rb, rb), rb)
        for h in range(nheads):
            ls = slice(h * GROUP, (h + 1) * GROUP)
            og = og_ref[rows, ls].astype(F32)
            y = _rms(o_scr[rows, ls]) * ng_ref[...] * (og * _sigmoid(og))
            y_ref[rows, ls] = y.astype(BF16)
        return carry

    lax.fori_loop(0, seq // rb, epilogue, 0)


def _hgrn_specs(seq):
    hw = HG_HEADS * GROUP
    big = lambda k: pl.BlockSpec((seq, hw), lambda b: (b, k), pipeline_mode=pl.Buffered(1))
    st_spec = pl.BlockSpec((1, HG_HEADS, GROUP, GROUP), lambda b: (b, 0, 0, 0))
    return hw, big, st_spec


def _hgrn_ctx_call(zc16, zcf, lbl, batch, seq):
    hw, big, st_spec = _hgrn_specs(seq)
    st_shape = jax.ShapeDtypeStruct((batch, HG_HEADS, GROUP, GROUP), F32)
    return pl.pallas_call(
        functools.partial(_hgrn_ctx_kernel, seq=seq),
        grid=(batch,),
        in_specs=[big(0), big(0), big(1), pl.BlockSpec((4, hw), lambda b: (0, 0))],
        out_specs=[st_spec, st_spec],
        out_shape=[st_shape, st_shape],
        scratch_shapes=[pltpu.VMEM((2 * HG_HEADS, GROUP, GROUP), F32)],
        name="hgrn_ctx",
        compiler_params=_params(("arbitrary",)),
    )(zc16, zcf, zcf, lbl)


def _hgrn_call(z16, zf, lbl, norm_g, sf, sb, batch, seq):
    hw, big, st_spec = _hgrn_specs(seq)
    return pl.pallas_call(
        functools.partial(_hgrn_kernel, seq=seq),
        grid=(batch,),
        in_specs=[big(1), big(2), big(0), big(1), big(3),
                  pl.BlockSpec((4, hw), lambda b: (0, 0)),
                  pl.BlockSpec((1, GROUP), lambda b: (0, 0)),
                  st_spec, st_spec],
        out_specs=pl.BlockSpec((seq, hw), lambda b: (b, 0)),
        out_shape=jax.ShapeDtypeStruct((batch * seq, hw), BF16),
        scratch_shapes=[pltpu.VMEM((2 * HG_HEADS, GROUP, GROUP), F32), pltpu.VMEM((seq, hw), F32)],
        name="hgrn",
        compiler_params=_params(("arbitrary",)),
    )(z16, z16, zf, zf, z16, lbl, norm_g, sf, sb)


def _keyfold_kernel(keys_ref, wqt_ref, o_ref):
    k_hi, k_lo = _hi_lo(keys_ref[0])
    w_hi, w_lo = _hi_lo(wqt_ref[...])
    o_ref[...] = (_dot(k_hi, w_hi) + (_dot(k_hi, w_lo) + _dot(k_lo, w_hi))).astype(BF16)


def _keyfold_call(keys, wqt):
    nhp, nk, c = keys.shape
    d = wqt.shape[1]
    return pl.pallas_call(
        _keyfold_kernel,
        grid=(nhp,),
        in_specs=[pl.BlockSpec((1, nk, c), lambda h: (h, 0, 0)),
                  pl.BlockSpec((c, d), lambda h: (h, 0))],
        out_specs=pl.BlockSpec((nk, d), lambda h: (h, 0)),
        out_shape=jax.ShapeDtypeStruct((nhp * nk, d), BF16),
        name="keyfold",
        compiler_params=_params(("arbitrary",)),
    )(keys, wqt)


def _merge_kernel(x_ref, yp_ref, yh_ref, gp_ref, gh_ref, g1_ref, sh_ref, sc_ref, ng_ref,
                  wa_ref, wb_ref, wo_ref, wqk_ref, x1_ref, h2_ref, st_ref):
    m = (_sigmoid(gp_ref[...].astype(F32)) * _dot(yp_ref[...], wa_ref[...])
         + _sigmoid(gh_ref[...].astype(F32)) * _dot(yh_ref[...], wb_ref[...]))
    x1 = x_ref[...] + g1_ref[0] * _dot(m.astype(BF16), wo_ref[...])
    x1_ref[...] = x1
    h2 = _rms(x1) * ng_ref[...] * (1.0 + sc_ref[0]) + sh_ref[0]
    h2_ref[...] = (h2 * _SQRT_HALF).astype(BF16)
    st = _dot(wqk_ref[...], h2.T.astype(BF16))
    for hp in range(2 * PEER_HEADS):
        st_ref[hp] = st[hp * PEER_NKEYS:(hp + 1) * PEER_NKEYS]


def _merge_call(x2d, ypool, yhg, z, mod3, norm2_g, wa16, wb16, wo16, wqk16, seq, tm):
    r, d = x2d.shape
    gcol0 = (z.shape[1] - 2 * d) // d
    modrow = lambda k: (lambda i: (k + N_MOD * ((i * tm) // seq), 0, 0))
    const2 = lambda i: (0, 0)
    nhp = 2 * PEER_HEADS
    return pl.pallas_call(
        _merge_kernel,
        grid=(r // tm,),
        in_specs=[pl.BlockSpec((tm, d), lambda i: (i, 0)),
                  pl.BlockSpec((tm, ypool.shape[1]), lambda i: (i, 0)),
                  pl.BlockSpec((tm, yhg.shape[1]), lambda i: (i, 0)),
                  pl.BlockSpec((tm, d), lambda i: (i, gcol0)),
                  pl.BlockSpec((tm, d), lambda i: (i, gcol0 + 1)),
                  pl.BlockSpec((1, 1, d), modrow(2)),
                  pl.BlockSpec((1, 1, d), modrow(3)),
                  pl.BlockSpec((1, 1, d), modrow(4)),
                  pl.BlockSpec((1, d), const2),
                  pl.BlockSpec(wa16.shape, const2),
                  pl.BlockSpec(wb16.shape, const2),
                  pl.BlockSpec(wo16.shape, const2),
                  pl.BlockSpec(wqk16.shape, const2)],
        out_specs=[pl.BlockSpec((tm, d), lambda i: (i, 0)),
                   pl.BlockSpec((tm, d), lambda i: (i, 0)),
                   pl.BlockSpec((nhp, PEER_NKEYS, tm), lambda i: (0, 0, i))],
        out_shape=[jax.ShapeDtypeStruct((r, d), F32),
                   jax.ShapeDtypeStruct((r, d), BF16),
                   jax.ShapeDtypeStruct((nhp, PEER_NKEYS, r), F32)],
        name="merge",
        compiler_params=_params(("arbitrary",)),
    )(x2d, ypool, yhg, z, z, mod3, mod3, mod3, norm2_g, wa16, wb16, wo16, wqk16)


def _top16(v, iota_n):
    vals, idxs = [], []
    for _ in range(PEER_TOPK):
        m = jnp.max(v, axis=0, keepdims=True)
        idx = jnp.min(jnp.where(v == m, iota_n, float(PEER_NKEYS)), axis=0, keepdims=True)
        v = jnp.where(iota_n == idx, -jnp.inf, v)
        vals.append(m)
        idxs.append(idx)
    return jnp.concatenate(vals, axis=0), jnp.concatenate(idxs, axis=0)


def _topk_unit(s1, s2):
    tg = s1.shape[1]
    kk = PEER_TOPK
    nkf = float(PEER_NKEYS)
    iota_n = lax.broadcasted_iota(I32, (PEER_NKEYS, tg), 0).astype(F32)
    v1, n1 = _top16(s1, iota_n)
    v2, n2 = _top16(s2, iota_n)

    r = lax.broadcasted_iota(I32, (8, tg), 0)
    rf = r.astype(F32)
    sel3 = lambda x: jnp.where(r < 5, x[0], x[1])
    sel4 = lambda x: jnp.where(r < 4, x[0], jnp.where(r < 6, x[1], x[2]))
    sel5 = lambda x: jnp.where(r < 2, x[0], x[1])

    def tiles(l1, l2):
        row1 = lambda a: l1[a:a + 1]
        lo2, hi1 = l2[0:8], l1[8:16]
        hi1_r2 = pltpu.roll(hi1, 2, 0)
        t1 = [row1(0), row1(0), row1(1), sel3((row1(2), row1(4))), sel4((row1(3), row1(5), row1(6))),
              sel5((row1(7), hi1_r2)), hi1_r2]
        t2 = [lo2, l2[8:16], lo2, sel3((lo2, pltpu.roll(lo2, 5, 0))),
              sel4((lo2, pltpu.roll(lo2, 4, 0), pltpu.roll(lo2, 6, 0))), sel5((lo2, l2[0:1])), l2[0:1]]
        return t1, t2

    tv1, tv2 = tiles(v1, v2)
    tn1, tn2 = tiles(n1, n2)
    cand = [a + b for a, b in zip(tv1, tv2)]
    cand[6] = jnp.where(r < 2, cand[6], -jnp.inf)
    cand = jnp.concatenate(cand, axis=0)
    eid = jnp.concatenate([a * nkf + b for a, b in zip(tn1, tn2)], axis=0)
    pos = jnp.concatenate([rf, rf + 8.0, rf + 16.0, sel3((rf + 32.0, rf + 59.0)),
                           sel4((rf + 48.0, rf + 76.0, rf + 90.0)), sel5((rf + 112.0, rf * 16.0 + 96.0)),
                           rf * 16.0 + 224.0], axis=0)
    tops, es = [], []
    for _ in range(kk):
        m = jnp.max(cand, axis=0, keepdims=True)
        p = jnp.min(jnp.where(cand == m, pos, 1e9), axis=0, keepdims=True)
        sel = pos == p
        es.append(jnp.max(jnp.where(sel, eid, -1.0), axis=0, keepdims=True))
        cand = jnp.where(sel, -jnp.inf, cand)
        tops.append(m)
    top = jnp.concatenate(tops, axis=0)
    e = jnp.concatenate(es, axis=0)
    ex = jnp.exp(top - top[0:1])
    gate = ex / jnp.sum(ex, axis=0, keepdims=True)
    e1 = jnp.floor(e * (1.0 / PEER_NKEYS))
    return e1, e - e1 * nkf, gate * _SQRT_HALF


def _peer_kernel(s_ref, h2_ref, ut_ref, v_ref, x1_ref, g2_ref, fg_ref, o_ref,
                 w_scr, stage_scr, tr0_scr, tr1_scr, tr2_scr, tr3_scr, acc_scr, rt_scr, rr_scr):
    tb, d = h2_ref.shape
    eb = v_ref.shape[0]
    i = pl.program_id(0)
    e = pl.program_id(1)
    nk = PEER_NKEYS
    ngrp = tb // GROUP
    pack = w_scr.shape[2]
    unroll = stage_scr.shape[0]
    tr_scrs = (tr0_scr, tr1_scr, tr2_scr, tr3_scr)
    units = s_ref.shape[0] // 2
    steps_per_grp = PEER_HEADS // units

    @pl.when((i > 0) & (e == 0))
    def _build():
        rd = lax.rem(i + 1, 2)
        for fld in range(3):
            for g in range(ngrp):
                rr_scr[fld, g * GROUP:(g + 1) * GROUP, :] = rt_scr[rd, fld, g].T
        iota_s = lax.broadcasted_iota(I32, (nk, nk), 0).astype(F32).astype(BF16)
        zero = jnp.zeros((), BF16)
        one = jnp.ones((), BF16)

        def flush(tt):
            for g in range(unroll // pack):
                blk = stage_scr[g * pack:(g + 1) * pack].astype(BF16)
                tr_scrs[g][...] = jnp.transpose(blk, (1, 0, 2))
            for g in range(unroll // pack):
                w_scr[tt * (unroll // pack) + g] = tr_scrs[g][...]

        def body(tt, carry):
            flush(jnp.maximum(tt - 1, 0))
            ts = [tt * unroll + r for r in range(unroll)]
            row = lambda fld, t: jnp.broadcast_to(rr_scr[fld, pl.ds(t, 1), :], (pack, nk)).astype(BF16)
            tiles = [iota_s[v * pack:(v + 1) * pack] for v in range(nk // pack)]
            onehot = lambda idx, val: jnp.concatenate([jnp.where(tl == idx, val, zero) for tl in tiles], axis=0)
            g1t = [onehot(row(0, t), row(2, t)) for t in ts]
            p2t = [onehot(row(1, t), one) for t in ts]
            for r, (g, p) in enumerate(zip(g1t, p2t)):
                stage_scr[r] = _dot_nt(g, p)
            return carry

        stage_scr[...] = jnp.zeros(stage_scr.shape, F32)
        ntrip = tb // unroll
        lax.fori_loop(0, ntrip, body, 0)
        flush(ntrip - 1)
        acc_scr[...] = jnp.zeros(acc_scr.shape, F32)

    grp = e // steps_per_grp
    head0 = lax.rem(e, steps_per_grp) * units
    wr = lax.rem(i, 2)
    n1_0 = e * (eb // nk)
    nchunk = eb // (2 * nk)

    def route(k):
        i1, i2, gate = _topk_unit(s_ref[2 * k], s_ref[2 * k + 1])
        rows = pl.ds(pl.multiple_of((head0 + k) * PEER_TOPK, PEER_TOPK), PEER_TOPK)
        rt_scr[wr, 0, grp, rows, :] = i1
        rt_scr[wr, 1, grp, rows, :] = i2
        rt_scr[wr, 2, grp, rows, :] = gate

    def activate(k2, pieces):
        a2 = _dot(h2_ref[...], ut_ref[:, 2 * k2 * nk:2 * (k2 + 1) * nk])
        for k in (2 * k2, 2 * k2 + 1):
            ak = a2[:, (k - 2 * k2) * nk:(k - 2 * k2 + 1) * nk]
            wn = w_scr[:, n1_0 + k, :, :].reshape(tb, nk)
            act = ak * (1.0 + lax.erf(ak))
            pieces.append(wn * act.astype(BF16))

    @pl.when(i == 0)
    def _route_only():
        for k in range(units):
            route(k)

    @pl.when(i > 0)
    def _route_and_evaluate():
        pieces = []
        per = nchunk // units
        for k in range(units):
            route(k)
            for k2 in range(k * per, (k + 1) * per):
                activate(k2, pieces)
        wg = jnp.concatenate(pieces, axis=1)
        tn = 256
        for j in range(d // tn):
            acc_scr[:, j * tn:(j + 1) * tn] += _dot(wg, v_ref[:, j * tn:(j + 1) * tn])

    @pl.when((i > 0) & (e == pl.num_programs(1) - 1))
    def _fin():
        x2 = x1_ref[...] + g2_ref[0] * acc_scr[...]
        o_ref[...] = _rms(x2) * fg_ref[...]


def _peer_call(st, h2, ut16, v16, x1, mod3, final_g, seq, tb, eb):
    r, d = h2.shape
    ne = v16.shape[0]
    nblk = r // tb
    ngrp = tb // GROUP
    nsteps = ne // eb
    units = PEER_HEADS * ngrp // nsteps
    assert units * nsteps == PEER_HEADS * ngrp and PEER_HEADS % units == 0
    steps_per_grp = PEER_HEADS // units
    slots = PEER_HEADS * PEER_TOPK
    pack = 16
    prev = lambda i: jnp.maximum(i - 1, 0)
    tok = lambda i, e: (prev(i), 0)
    return pl.pallas_call(
        _peer_kernel,
        grid=(nblk + 1, nsteps),
        in_specs=[pl.BlockSpec((2 * units, PEER_NKEYS, GROUP),
                               lambda i, e: (lax.rem(e, steps_per_grp), 0,
                                             jnp.minimum(i, nblk - 1) * ngrp + e // steps_per_grp)),
                  pl.BlockSpec((tb, d), tok),
                  pl.BlockSpec((d, eb), lambda i, e: (0, e)),
                  pl.BlockSpec((eb, d), lambda i, e: (e, 0)),
                  pl.BlockSpec((tb, d), tok),
                  pl.BlockSpec((1, 1, d), lambda i, e: (5 + N_MOD * ((prev(i) * tb) // seq), 0, 0)),
                  pl.BlockSpec((1, d), lambda i, e: (0, 0))],
        out_specs=pl.BlockSpec((tb, d), tok),
        out_shape=jax.ShapeDtypeStruct((r, d), F32),
        scratch_shapes=[pltpu.VMEM((tb // pack, PEER_NKEYS, pack, PEER_NKEYS), BF16),
                        pltpu.VMEM((4 * pack, PEER_NKEYS, PEER_NKEYS), F32),
                        pltpu.VMEM((PEER_NKEYS, pack, PEER_NKEYS), BF16),
                        pltpu.VMEM((PEER_NKEYS, pack, PEER_NKEYS), BF16),
                        pltpu.VMEM((PEER_NKEYS, pack, PEER_NKEYS), BF16),
                        pltpu.VMEM((PEER_NKEYS, pack, PEER_NKEYS), BF16),
                        pltpu.VMEM((tb, d), F32),
                        pltpu.VMEM((2, 3, ngrp, slots, GROUP), F32),
                        pltpu.VMEM((3, tb, slots), F32)],
        name="peer",
        compiler_params=_params(("arbitrary", "arbitrary")),
    )(st, h2, ut16, v16, x1, mod3, final_g)


def kernel(x, c, ctx, c_ctx, w_mod, b_mod, norm1_g, norm2_g, w_in, pool_w, pool_scale, hg_lb_logits, hg_norm_g,
           w_a, w_b, w_o, peer_wq, peer_keys, peer_u, peer_v, final_g):
    batch, seq, d = x.shape
    ctx_len = ctx.shape[1]
    assert w_mod.shape[0] == 1 and seq == GRID_W * GRID_W and hg_lb_logits.shape[1] == 2
    hw = HG_HEADS * GROUP

    cvec = jnp.concatenate([c, c_ctx[None, :], jnp.zeros((8 - batch - 1, d), F32)], axis=0)
    mod = _mod_call(cvec, w_mod[0], b_mod[0][None, :])
    mod3 = mod.reshape(8 * N_MOD, 1, d)

    w_in16 = w_in[0].astype(BF16)
    f0, f1 = 3 * hw, 5 * hw
    w_in16 = jnp.concatenate([w_in16[:, :f0], w_in16[:, f1:], w_in16[:, f0:f1]], axis=1)
    g1n = norm1_g[0][None, :]
    x2d = x.reshape(batch * seq, d)
    z16, zf = _inproj_call(x2d, mod3, 0, 1, seq, g1n, w_in16, TM_INPROJ, 2 * hw)
    w_ctx16 = jnp.concatenate([w_in16[:, 2 * hw:4 * hw], w_in16[:, -2 * hw:]], axis=1)
    zc16, zcf = _inproj_call(ctx.reshape(batch * ctx_len, d), mod3, N_MOD * batch, N_MOD * batch + 1, None, g1n,
                             w_ctx16, TM_CTX, 2 * hw)

    lbl = hg_lb_logits.reshape(2 * hg_lb_logits.shape[1], hw)
    sf, sb = _hgrn_ctx_call(zc16, zcf, lbl, batch, ctx_len)
    yhg = _hgrn_call(z16, zf, lbl, hg_norm_g[0][None, :], sf, sb, batch, seq)
    ypool = _pool_call(z16, pool_w[0], pool_scale[0][None, :], batch, seq)

    nhp = 2 * PEER_HEADS
    wqk16 = _keyfold_call(peer_keys[0].reshape(nhp, PEER_NKEYS, GROUP), peer_wq[0].T)
    x1, h2, st = _merge_call(x2d, ypool, yhg, z16, mod3, norm2_g[0][None, :],
                             w_a[0].astype(BF16), w_b[0].astype(BF16), w_o[0].astype(BF16), wqk16, seq, TM_MERGE)
    out = _peer_call(st, h2, peer_u[0].T.astype(BF16), peer_v[0].astype(BF16), x1, mod3,
                     final_g[None, :], seq, TB_PEER, EB_PEER)
    return out.reshape(batch, seq, d)
```
